```python
import math
import jax, jax.numpy as jnp
from jax import lax
import numpy as np

D_MODEL = 2048
BATCH = 4
SEQ = 2048
DEPTH = 1
DEC_BATCH = 128
DEC_SEQ = 4
PAST_LEN = 16384
PAGE_SIZE = 128

N_META = 16
N_HEADS = 16
QK_NOPE = 128
QK_ROPE = 64
QK_HEAD = QK_NOPE + QK_ROPE
V_HEAD = 128
KV_RANK = 512
ATTN_WIDTH = N_HEADS * V_HEAD
CONV_WIDTH = D_MODEL
CONV_K = 31
ROPE_THETA = 10000.0
NORM_EPS = 1e-6
Q_BLOCK = 128
ATTN_SCALE = 1.0 / math.sqrt(QK_HEAD)
Q_COLS = N_HEADS * QK_HEAD
IN_COLS = Q_COLS + KV_RANK + QK_ROPE + ATTN_WIDTH + 2 * CONV_WIDTH + CONV_WIDTH + 2 * D_MODEL

kernel_name = "mla_conformer_gated_hybrid_step"


def rms_norm(x, g):
    xf = x.astype(jnp.float32)
    y = xf * lax.rsqrt(jnp.mean(xf * xf, axis=-1, keepdims=True) + NORM_EPS)
    return (y * g.astype(jnp.float32)).astype(x.dtype)


def layer_norm(x, g, b):
    xf = x.astype(jnp.float32)
    mu = jnp.mean(xf, axis=-1, keepdims=True)
    xc = xf - mu
    y = xc * lax.rsqrt(jnp.mean(xc * xc, axis=-1, keepdims=True) + NORM_EPS)
    return (y * g.astype(jnp.float32) + b.astype(jnp.float32)).astype(x.dtype)


def rope_tables(pos):
    inv = 1.0 / (ROPE_THETA ** (jnp.arange(0, QK_ROPE, 2, dtype=jnp.float32) / QK_ROPE))
    ang = pos.astype(jnp.float32)[:, None] * inv[None, :]
    return jnp.cos(ang), jnp.sin(ang)


def apply_rope(x, cos, sin):
    xf = x.astype(jnp.float32)
    half = QK_ROPE // 2
    x1, x2 = xf[..., :half], xf[..., half:]
    return jnp.concatenate([x1 * cos - x2 * sin, x2 * cos + x1 * sin], axis=-1).astype(x.dtype)


def split_in(p):
    a = Q_COLS
    b = a + KV_RANK
    c = b + QK_ROPE
    d = c + ATTN_WIDTH
    e = d + 2 * CONV_WIDTH
    f = e + CONV_WIDTH
    return jnp.split(p, [a, b, c, d, e, f], axis=-1)


def project_inputs(hn, w_in, w_kv_norm, cos, sin):
    B, T, _ = hn.shape
    proj = jnp.einsum('btd,dc->btc', hn, w_in)
    q, ckv, kr, gate_a, glu, gate_c, mgate = split_in(proj)
    q = q.reshape(B, T, N_HEADS, QK_HEAD)
    q_nope = q[..., :QK_NOPE]
    q_rope = apply_rope(q[..., QK_NOPE:], cos[None, :, None, :], sin[None, :, None, :])
    ckv = rms_norm(ckv, w_kv_norm)
    kr = apply_rope(kr, cos[None], sin[None])
    glu_a, glu_b = jnp.split(glu, 2, axis=-1)
    u = glu_a * jax.nn.sigmoid(glu_b)
    return q_nope, q_rope, ckv, kr, gate_a, u, gate_c, mgate


def attn_prompt(q_nope, q_rope, ckv, kr, w_uk, w_uv):
    B, T = ckv.shape[:2]
    k_nope = jnp.einsum('btr,rhd->bthd', ckv, w_uk)
    v = jnp.einsum('btr,rhd->bthd', ckv, w_uv)
    nb = -(-T // Q_BLOCK)
    t_pad = nb * Q_BLOCK
    pad = t_pad - T
    padt = lambda a: jnp.pad(a, [(0, 0), (0, pad)] + [(0, 0)] * (a.ndim - 2))
    k_nope, v, kr_p = padt(k_nope), padt(v), padt(kr)
    qn_b = padt(q_nope).reshape(B, nb, Q_BLOCK, N_HEADS, QK_NOPE).swapaxes(0, 1)
    qr_b = padt(q_rope).reshape(B, nb, Q_BLOCK, N_HEADS, QK_ROPE).swapaxes(0, 1)
    k_pos = jnp.arange(t_pad)

    def block(args):
        qn, qr, i = args
        q_pos = i * Q_BLOCK + jnp.arange(Q_BLOCK)
        s = (jnp.einsum('bqhd,bkhd->bhqk', qn, k_nope)
             + jnp.einsum('bqhp,bkp->bhqk', qr, kr_p)).astype(jnp.float32) * ATTN_SCALE
        s = jnp.where(k_pos[None, :] <= q_pos[:, None], s, -jnp.inf)
        p = jax.nn.softmax(s, axis=-1).astype(v.dtype)
        return jnp.einsum('bhqk,bkhd->bqhd', p, v)

    out = lax.map(block, (qn_b, qr_b, jnp.arange(nb)))
    return out.swapaxes(0, 1).reshape(B, t_pad, N_HEADS, V_HEAD)[:, :T]


def attn_sample(q_nope, q_rope, ckv_new, kr_new, cache_ckv, cache_kr, page_table, w_uk, w_uv):
    S = q_nope.shape[1]
    q_lat = jnp.einsum('bshd,rhd->bshr', q_nope, w_uk)
    causal = jnp.arange(S)[None, :] <= jnp.arange(S)[:, None]

    def one(args):
        pt, ql, qr, cn, kn = args
        ckv_p = cache_ckv[pt].reshape(-1, KV_RANK)
        kr_p = cache_kr[pt].reshape(-1, QK_ROPE)
        n_past = ckv_p.shape[0]
        s_past = (jnp.einsum('shr,tr->hst', ql, ckv_p)
                  + jnp.einsum('shp,tp->hst', qr, kr_p)).astype(jnp.float32)
        s_new = (jnp.einsum('shr,tr->hst', ql, cn)
                 + jnp.einsum('shp,tp->hst', qr, kn)).astype(jnp.float32)
        s_new = jnp.where(causal[None], s_new, -jnp.inf)
        s = jnp.concatenate([s_past, s_new], axis=-1) * ATTN_SCALE
        p = jax.nn.softmax(s, axis=-1).astype(ckv_p.dtype)
        return (jnp.einsum('hst,tr->shr', p[..., :n_past], ckv_p)
                + jnp.einsum('hst,tr->shr', p[..., n_past:], cn))

    o_lat = lax.map(one, (page_table, q_lat, q_rope, ckv_new, kr_new))
    return jnp.einsum('bshr,rhd->bshd', o_lat, w_uv)


def depthwise_conv(u_ext, w_dw, b_dw):
    C = u_ext.shape[-1]
    out = lax.conv_general_dilated(u_ext, w_dw[:, None, :].astype(u_ext.dtype), window_strides=(1,),
                                   padding='VALID', dimension_numbers=('NWC', 'WIO', 'NWC'),
                                   feature_group_count=C)
    return out + b_dw.astype(out.dtype)


def mixer_layer(h, cos, sin, attn_fn, conv_prefix, w_pre, w_post, w_in, w_kv_norm, w_o_attn,
                w_dw, b_dw, w_ln, b_ln, w_o_conv, w_out):
    B, T, _ = h.shape
    hn = rms_norm(h, w_pre)
    q_nope, q_rope, ckv, kr, gate_a, u, gate_c, mgate = project_inputs(hn, w_in, w_kv_norm, cos, sin)
    attn = attn_fn(q_nope, q_rope, ckv, kr)
    ya = jnp.einsum('btc,cd->btd', attn.reshape(B, T, ATTN_WIDTH) * jax.nn.silu(gate_a), w_o_attn)
    u_ext = jnp.concatenate([conv_prefix.astype(u.dtype), u], axis=1)
    c = jax.nn.silu(layer_norm(depthwise_conv(u_ext, w_dw, b_dw), w_ln, b_ln))
    yc = jnp.einsum('btc,cd->btd', c * jax.nn.silu(gate_c), w_o_conv)
    g_a, g_c = jnp.split(jax.nn.sigmoid(mgate), 2, axis=-1)
    y = jnp.einsum('btd,de->bte', g_a * ya + g_c * yc, w_out)
    h = h + rms_norm(y, w_post)
    return h, ckv, kr, u_ext[:, -(CONV_K - 1):]


def setup_inputs(seed: int = 0) -> dict:
    key = jax.random.key(seed)
    ks = jax.random.split(key, 24)
    f32 = jnp.float32
    n_pages = PAST_LEN // PAGE_SIZE
    n_used = DEC_BATCH * n_pages
    n_pool = n_used + max(1, n_used // 4)
    nrm = lambda k, shape, s: jax.random.normal(k, shape, f32) * s
    page_table = jax.random.permutation(ks[5], n_pool)[:n_used].reshape(DEC_BATCH, n_pages).astype(jnp.int32)
    return {
        "x_prompt": nrm(ks[0], (BATCH, SEQ, D_MODEL), 1.0),
        "x_sample": nrm(ks[1], (DEC_BATCH, DEC_SEQ, D_MODEL), 1.0),
        "cache_ckv": nrm(ks[2], (DEPTH, n_pool, PAGE_SIZE, KV_RANK), 1.0),
        "cache_krope": nrm(ks[3], (DEPTH, n_pool, PAGE_SIZE, QK_ROPE), 1.0),
        "state_conv": nrm(ks[4], (DEPTH, DEC_BATCH, CONV_K - 1, CONV_WIDTH), 0.5),
        "page_table": page_table,
        "meta_tokens": nrm(ks[6], (N_META, D_MODEL), 1.0),
        "w_norm_pre": 1.0 + nrm(ks[7], (DEPTH, D_MODEL), 0.02),
        "w_norm_post": 1.0 + nrm(ks[8], (DEPTH, D_MODEL), 0.02),
        "w_in": nrm(ks[9], (DEPTH, D_MODEL, IN_COLS), D_MODEL ** -0.5),
        "w_kv_norm": 1.0 + nrm(ks[10], (DEPTH, KV_RANK), 0.02),
        "w_uk": nrm(ks[11], (DEPTH, KV_RANK, N_HEADS, QK_NOPE), KV_RANK ** -0.5),
        "w_uv": nrm(ks[12], (DEPTH, KV_RANK, N_HEADS, V_HEAD), KV_RANK ** -0.5),
        "w_o_attn": nrm(ks[13], (DEPTH, ATTN_WIDTH, D_MODEL), ATTN_WIDTH ** -0.5),
        "w_dw": nrm(ks[14], (DEPTH, CONV_K, CONV_WIDTH), CONV_K ** -0.5),
        "b_dw": nrm(ks[15], (DEPTH, CONV_WIDTH), 0.02),
        "w_conv_ln": 1.0 + nrm(ks[16], (DEPTH, CONV_WIDTH), 0.02),
        "b_conv_ln": nrm(ks[17], (DEPTH, CONV_WIDTH), 0.02),
        "w_o_conv": nrm(ks[18], (DEPTH, CONV_WIDTH, D_MODEL), CONV_WIDTH ** -0.5),
        "w_out": nrm(ks[19], (DEPTH, D_MODEL, D_MODEL), D_MODEL ** -0.5),
    }


def reference(x_prompt, x_sample, cache_ckv, cache_krope, state_conv, page_table, meta_tokens,
              w_norm_pre, w_norm_post, w_in, w_kv_norm, w_uk, w_uv, w_o_attn, w_dw, b_dw,
              w_conv_ln, b_conv_ln, w_o_conv, w_out):
    B = x_prompt.shape[0]
    meta = jnp.broadcast_to(meta_tokens[None].astype(x_prompt.dtype), (B, N_META, D_MODEL))
    hp = jnp.concatenate([meta, x_prompt], axis=1)
    T = hp.shape[1]
    hs = x_sample
    S = hs.shape[1]
    cos_p, sin_p = rope_tables(jnp.arange(T))
    cos_s, sin_s = rope_tables(PAST_LEN + jnp.arange(S))
    conv_zero = jnp.zeros((B, CONV_K - 1, CONV_WIDTH), hp.dtype)
    ckv_p_l, kr_p_l, conv_p_l, ckv_s_l, kr_s_l, conv_s_l = [], [], [], [], [], []
    for l in range(DEPTH):
        lw = (w_norm_pre[l], w_norm_post[l], w_in[l], w_kv_norm[l], w_o_attn[l], w_dw[l], b_dw[l],
              w_conv_ln[l], b_conv_ln[l], w_o_conv[l], w_out[l])
        uk, uv = w_uk[l], w_uv[l]
        fn_p = lambda qn, qr, ck, kk, uk=uk, uv=uv: attn_prompt(qn, qr, ck, kk, uk, uv)
        hp, ckv_p, kr_p, conv_p = mixer_layer(hp, cos_p, sin_p, fn_p, conv_zero, *lw)
        cc, cr = cache_ckv[l], cache_krope[l]
        fn_s = lambda qn, qr, ck, kk, uk=uk, uv=uv, cc=cc, cr=cr: attn_sample(qn, qr, ck, kk, cc, cr, page_table, uk, uv)
        hs, ckv_s, kr_s, conv_s = mixer_layer(hs, cos_s, sin_s, fn_s, state_conv[l], *lw)
        ckv_p_l.append(ckv_p); kr_p_l.append(kr_p); conv_p_l.append(conv_p)
        ckv_s_l.append(ckv_s); kr_s_l.append(kr_s); conv_s_l.append(conv_s)
    y_prompt = hp[:, N_META:]
    y_sample = hs
    return (y_prompt, y_sample, jnp.stack(ckv_p_l), jnp.stack(kr_p_l), jnp.stack(conv_p_l),
            jnp.stack(ckv_s_l), jnp.stack(kr_s_l), jnp.stack(conv_s_l))
```

```python
import functools
import math

import jax
import jax.numpy as jnp
from jax import lax
from jax.experimental import pallas as pl
from jax.experimental.pallas import tpu as pltpu

F32 = jnp.float32
BF16 = jnp.bfloat16

D_MODEL = 2048
N_META = 16
N_HEADS = 16
QK_NOPE = 128
QK_ROPE = 64
QK_HEAD = QK_NOPE + QK_ROPE
V_HEAD = 128
KV_RANK = 512
ATTN_WIDTH = N_HEADS * V_HEAD
CONV_WIDTH = D_MODEL
CONV_K = 31
ROPE_THETA = 10000.0
NORM_EPS = 1e-6
ATTN_SCALE = 1.0 / math.sqrt(QK_HEAD)
Q_COLS = N_HEADS * QK_HEAD

LANES = 128
QK_PAD = 2 * LANES
T_TILE = 256
PAGES_PER_STEP = 16
NEW_PAD = 16
VMEM_LIMIT = 56 * 1024 * 1024


def _cparams(sem):
    return pltpu.CompilerParams(dimension_semantics=sem, vmem_limit_bytes=VMEM_LIMIT)


def _rmsnorm_body(x_ref, g_ref, o_ref):
    x = x_ref[...]
    y = x * lax.rsqrt(jnp.mean(x * x, axis=-1, keepdims=True) + NORM_EPS)
    o_ref[...] = (y * g_ref[...]).astype(o_ref.dtype)


def _rmsnorm(x, g, tm):
    rows, d = x.shape
    return pl.pallas_call(
        _rmsnorm_body,
        grid=(rows // tm,),
        in_specs=[pl.BlockSpec((tm, d), lambda i: (i, 0)), pl.BlockSpec((1, d), lambda i: (0, 0))],
        out_specs=pl.BlockSpec((tm, d), lambda i: (i, 0)),
        out_shape=jax.ShapeDtypeStruct((rows, d), BF16),
        compiler_params=_cparams(("parallel",)),
        name="rmsnorm_pre",
    )(x, g.reshape(1, d))


def _proj_body(epilogue, n_w, n_aux, x_ref, *refs):
    w_refs, aux_refs, out_refs = refs[:n_w], refs[n_w:n_w + n_aux], refs[n_w + n_aux:]
    x = x_ref[...].astype(BF16)
    accs = [jnp.dot(x, w[...], preferred_element_type=F32) for w in w_refs]
    outs = epilogue(accs, [a[...] for a in aux_refs])
    for o_ref, o in zip(out_refs, outs):
        o_ref[...] = o.astype(o_ref.dtype)


def _proj(x, ws, epilogue, outs, *, tm, tns, vec_aux=(), row_aux=(), name):
    rows, k = x.shape
    grid = (rows // tm, ws[0].shape[1] // tns[0])
    in_specs = [pl.BlockSpec((tm, k), lambda i, j: (i, 0))]
    in_specs += [pl.BlockSpec((k, tn), lambda i, j: (0, j)) for tn in tns]
    in_specs += [pl.BlockSpec((1, a.shape[1]), lambda i, j: (0, 0)) for a in vec_aux]
    in_specs += [pl.BlockSpec((tm, a.shape[1]), lambda i, j: (i, 0)) for a in row_aux]
    out_specs = [pl.BlockSpec((tm, tw), lambda i, j: (i, j)) for (_, tw, _) in outs]
    out_shape = [jax.ShapeDtypeStruct((rows, w), dt) for (w, _, dt) in outs]
    return pl.pallas_call(
        functools.partial(_proj_body, epilogue, len(ws), len(vec_aux) + len(row_aux)),
        grid=grid, in_specs=in_specs, out_specs=out_specs, out_shape=out_shape,
        compiler_params=_cparams(("parallel", "arbitrary")),
        name=name,
    )(x, *ws, *vec_aux, *row_aux)


def _swap_rope_halves(x):
    lane = lax.broadcasted_iota(jnp.int32, x.shape, 1)
    first_half = (lane % QK_ROPE) < (QK_ROPE // 2)
    return jnp.where(first_half, pltpu.roll(x, LANES - QK_ROPE // 2, 1), pltpu.roll(x, QK_ROPE // 2, 1))


def _rope128(x, cos_t, sin_t):
    return x * cos_t + _swap_rope_halves(x) * sin_t


def _ep_q(heads_per_tile, accs, aux):
    acc, = accs
    cos_t, sin_t = aux
    parts = []
    for h in range(heads_per_tile):
        parts.append(acc[:, h * QK_PAD:h * QK_PAD + LANES])
        parts.append(_rope128(acc[:, h * QK_PAD + LANES:(h + 1) * QK_PAD], cos_t, sin_t))
    return [jnp.concatenate(parts, axis=1)]


def _ep_ckv_kr(accs, aux):
    c, kr = accs
    g, cos_t, sin_t = aux
    ckv = c * lax.rsqrt(jnp.mean(c * c, axis=-1, keepdims=True) + NORM_EPS) * g
    kr = _rope128(kr, cos_t, sin_t)
    return [ckv, kr[:, :QK_ROPE], kr]


def _ep_silu(accs, aux):
    a, = accs
    return [a * jax.nn.sigmoid(a)]


def _ep_glu(accs, aux):
    a, b = accs
    return [a * jax.nn.sigmoid(b)]


def _ep_sigmoid2(accs, aux):
    a, b = accs
    return [jax.nn.sigmoid(a), jax.nn.sigmoid(b)]


def _ep_kv_up(heads_per_tile, accs, aux):
    kn, v = accs
    kr_pad, = aux
    parts = []
    for h in range(heads_per_tile):
        parts.append(kn[:, h * QK_NOPE:(h + 1) * QK_NOPE].astype(BF16))
        parts.append(kr_pad)
    return [jnp.concatenate(parts, axis=1), v]


def _head_mm_body(x_ref, w_ref, o_ref):
    o_ref[...] = jnp.dot(x_ref[...].astype(BF16), w_ref[0], preferred_element_type=F32).astype(o_ref.dtype)


def _head_mm_gated_body(x_ref, w_ref, g_ref, o_ref):
    y = jnp.dot(x_ref[...].astype(BF16), w_ref[0], preferred_element_type=F32)
    o_ref[...] = (y * g_ref[...].astype(F32)).astype(o_ref.dtype)


def _head_mm(x, w, gate=None, *, out_dtype, name):
    rows = x.shape[0]
    nh, k, n = w.shape
    in_specs = [pl.BlockSpec((rows, k), lambda h: (0, h)), pl.BlockSpec((1, k, n), lambda h: (h, 0, 0))]
    args = [x, w]
    body = _head_mm_body
    if gate is not None:
        in_specs.append(pl.BlockSpec((rows, n), lambda h: (0, h)))
        args.append(gate)
        body = _head_mm_gated_body
    return pl.pallas_call(
        body, grid=(nh,), in_specs=in_specs,
        out_specs=pl.BlockSpec((rows, n), lambda h: (0, h)),
        out_shape=jax.ShapeDtypeStruct((rows, nh * n), out_dtype),
        compiler_params=_cparams(("parallel",)),
        name=name,
    )(*args)


def _prompt_attn_body(q_ref, k_ref, v_ref, g_ref, o_ref, *, tq, tk):
    qi = pl.program_id(2)
    q = q_ref[...]
    q_pos = qi * tq + lax.broadcasted_iota(jnp.int32, (tq, tk), 0)
    k_off = lax.broadcasted_iota(jnp.int32, (tq, tk), 1)

    def step(ki, carry):
        m, l, acc = carry
        ks = pl.multiple_of(ki * tk, tk)
        k = k_ref[pl.ds(ks, tk), :]
        v = v_ref[pl.ds(ks, tk), :]
        s = lax.dot_general(q, k, (((1,), (1,)), ((), ())), preferred_element_type=F32) * ATTN_SCALE
        s = jnp.where(ks + k_off <= q_pos, s, -jnp.inf)
        m_new = jnp.maximum(m, jnp.max(s, axis=-1, keepdims=True))
        alpha = jnp.exp(m - m_new)
        p = jnp.exp(s - m_new)
        l = alpha * l + jnp.sum(p, axis=-1, keepdims=True)
        acc = alpha * acc + jnp.dot(p.astype(BF16), v, preferred_element_type=F32)
        return m_new, l, acc

    init = (jnp.full((tq, 1), -jnp.inf, F32), jnp.zeros((tq, 1), F32), jnp.zeros((tq, V_HEAD), F32))
    _, l, acc = lax.fori_loop(0, (qi + 1) * (tq // tk), step, init)
    o_ref[...] = (acc / l * g_ref[...].astype(F32)).astype(o_ref.dtype)


def _prompt_attn(q256, k256, v, gate, *, batch, t_pad, tq, tk):
    nq = t_pad // tq
    return pl.pallas_call(
        functools.partial(_prompt_attn_body, tq=tq, tk=tk),
        grid=(batch, N_HEADS, nq),
        in_specs=[
            pl.BlockSpec((tq, QK_PAD), lambda b, h, i: (b * nq + i, h)),
            pl.BlockSpec((t_pad, QK_PAD), lambda b, h, i: (b, h)),
            pl.BlockSpec((t_pad, V_HEAD), lambda b, h, i: (b, h)),
            pl.BlockSpec((tq, V_HEAD), lambda b, h, i: (b * nq + i, h)),
        ],
        out_specs=pl.BlockSpec((tq, V_HEAD), lambda b, h, i: (b * nq + i, h)),
        out_shape=jax.ShapeDtypeStruct((batch * t_pad, ATTN_WIDTH), BF16),
        compiler_params=_cparams(("parallel", "parallel", "arbitrary")),
        name="prompt_attn",
    )(q256, k256, v, gate)


def _sample_attn_body(pt_ref, ql_ref, qr_ref, cn_ref, kn_ref, *refs, n_pages, n_new):
    c_refs = refs[:n_pages]
    r_refs = refs[n_pages:2 * n_pages]
    o_ref = refs[2 * n_pages]
    cbuf, rbuf, m_sc, l_sc, acc_sc = refs[2 * n_pages + 1:]
    step = pl.program_id(1)
    ql = ql_ref[0]
    qr = qr_ref[0]
    n_rows = ql.shape[0]
    dn_t = (((1,), (1,)), ((), ()))

    @pl.when(step == 0)
    def _():
        cn = cn_ref[0]
        kn = kn_ref[0]
        s = (lax.dot_general(ql, cn, dn_t, preferred_element_type=F32)
             + lax.dot_general(qr, kn, dn_t, preferred_element_type=F32)) * ATTN_SCALE
        tok = lax.broadcasted_iota(jnp.int32, s.shape, 0) // N_HEADS
        key = lax.broadcasted_iota(jnp.int32, s.shape, 1)
        s = jnp.where(key <= tok, s, -jnp.inf)
        m = jnp.max(s, axis=-1, keepdims=True)
        p = jnp.exp(s - m)
        m_sc[...] = m
        l_sc[...] = jnp.sum(p, axis=-1, keepdims=True)
        acc_sc[...] = jnp.dot(p.astype(BF16), cn, preferred_element_type=F32)

    for k in range(n_pages):
        cbuf[k * 128:(k + 1) * 128, :] = c_refs[k][0, 0].astype(BF16)
        rbuf[k * 128:(k + 1) * 128, :] = r_refs[k][0, 0].astype(BF16)
    c = cbuf[...]
    s = (lax.dot_general(ql, c, dn_t, preferred_element_type=F32)
         + lax.dot_general(qr, rbuf[...], dn_t, preferred_element_type=F32)) * ATTN_SCALE
    m_old = m_sc[...]
    m_new = jnp.maximum(m_old, jnp.max(s, axis=-1, keepdims=True))
    alpha = jnp.exp(m_old - m_new)
    p = jnp.exp(s - m_new)
    m_sc[...] = m_new
    l_sc[...] = alpha * l_sc[...] + jnp.sum(p, axis=-1, keepdims=True)
    acc_sc[...] = alpha * acc_sc[...] + jnp.dot(p.astype(BF16), c, preferred_element_type=F32)

    @pl.when(step == pl.num_programs(1) - 1)
    def _():
        o_ref[0] = (acc_sc[...] / l_sc[...]).astype(o_ref.dtype)


def _sample_attn(page_table, q_lat, q_rope, cn_pad, kn_pad, cache_c, cache_r, *, n_new):
    nb, n_rows, _ = q_lat.shape
    n_tbl = page_table.shape[1]
    npg = PAGES_PER_STEP
    n_steps = n_tbl // npg
    page = cache_c.shape[1]

    def q_map(b, s, pt):
        return (b, 0, 0)

    def page_map(k):
        return lambda b, s, pt: (0, pt[b, s * npg + k], 0, 0)

    cache_c4 = cache_c[None]
    cache_r4 = cache_r[None]
    in_specs = [
        pl.BlockSpec((1, n_rows, KV_RANK), q_map),
        pl.BlockSpec((1, n_rows, QK_ROPE), q_map),
        pl.BlockSpec((1, NEW_PAD, KV_RANK), q_map),
        pl.BlockSpec((1, NEW_PAD, QK_ROPE), q_map),
    ]
    in_specs += [pl.BlockSpec((1, 1, page, KV_RANK), page_map(k)) for k in range(npg)]
    in_specs += [pl.BlockSpec((1, 1, page, QK_ROPE), page_map(k)) for k in range(npg)]
    grid_spec = pltpu.PrefetchScalarGridSpec(
        num_scalar_prefetch=1,
        grid=(nb, n_steps),
        in_specs=in_specs,
        out_specs=pl.BlockSpec((1, n_rows, KV_RANK), q_map),
        scratch_shapes=[
            pltpu.VMEM((npg * page, KV_RANK), BF16),
            pltpu.VMEM((npg * page, QK_ROPE), BF16),
            pltpu.VMEM((n_rows, 1), F32),
            pltpu.VMEM((n_rows, 1), F32),
            pltpu.VMEM((n_rows, KV_RANK), F32),
        ],
    )
    return pl.pallas_call(
        functools.partial(_sample_attn_body, n_pages=npg, n_new=n_new),
        grid_spec=grid_spec,
        out_shape=jax.ShapeDtypeStruct((nb, n_rows, KV_RANK), F32),
        compiler_params=_cparams(("parallel", "arbitrary")),
        name="sample_attn",
    )(page_table, q_lat, q_rope, cn_pad, kn_pad, *([cache_c4] * npg), *([cache_r4] * npg))


CONV_HALO = 32
CONV_CHUNK = 128


def _prompt_conv_body(u_ref, w_ref, b_ref, o_ref, ext, *, t_pad):
    tc = u_ref.shape[1]
    ext[0:CONV_HALO, :] = jnp.zeros((CONV_HALO, tc), F32)
    ext[CONV_HALO:, :] = u_ref[...]
    off = CONV_HALO - (CONV_K - 1)
    for c in range(t_pad // CONV_CHUNK):
        base = c * CONV_CHUNK + off
        acc = jnp.broadcast_to(b_ref[...], (CONV_CHUNK, tc))
        for k in range(CONV_K):
            acc = acc + ext[base + k:base + k + CONV_CHUNK, :] * w_ref[k:k + 1, :]
        o_ref[c * CONV_CHUNK:(c + 1) * CONV_CHUNK, :] = acc


def _prompt_conv(u, w_dw, b_dw, *, batch, t_pad, tc):
    c = u.shape[1]
    return pl.pallas_call(
        functools.partial(_prompt_conv_body, t_pad=t_pad),
        grid=(batch, c // tc),
        in_specs=[
            pl.BlockSpec((t_pad, tc), lambda b, j: (b, j)),
            pl.BlockSpec((CONV_K, tc), lambda b, j: (0, j)),
            pl.BlockSpec((1, tc), lambda b, j: (0, j)),
        ],
        out_specs=pl.BlockSpec((t_pad, tc), lambda b, j: (b, j)),
        out_shape=jax.ShapeDtypeStruct(u.shape, F32),
        scratch_shapes=[pltpu.VMEM((CONV_HALO + t_pad, tc), F32)],
        compiler_params=_cparams(("parallel", "parallel")),
        name="prompt_conv",
    )(u, w_dw, b_dw.reshape(1, c))


def _sample_conv_body(st_ref, u_ref, ws_ref, wu_ref, b_ref, o_ref, *, n_new):
    st = st_ref[...]
    u = u_ref[...]
    for t in range(n_new):
        y = jnp.sum(st * ws_ref[t][None], axis=1) + jnp.sum(u * wu_ref[t][None], axis=1)
        o_ref[:, t, :] = y + b_ref[...]


def _sample_conv(state, u, w_dw, b_dw, *, bb, tc):
    nb, hist, c = state.shape
    n_new = u.shape[1]
    w_ext = jnp.stack([jnp.pad(w_dw, ((t, n_new - 1 - t), (0, 0))) for t in range(n_new)])
    w_state = w_ext[:, :hist]
    w_new = w_ext[:, hist:]
    return pl.pallas_call(
        functools.partial(_sample_conv_body, n_new=n_new),
        grid=(nb // bb, c // tc),
        in_specs=[
            pl.BlockSpec((bb, hist, tc), lambda i, j: (i, 0, j)),
            pl.BlockSpec((bb, n_new, tc), lambda i, j: (i, 0, j)),
            pl.BlockSpec((n_new, hist, tc), lambda i, j: (0, 0, j)),
            pl.BlockSpec((n_new, n_new, tc), lambda i, j: (0, 0, j)),
            pl.BlockSpec((1, tc), lambda i, j: (0, j)),
        ],
        out_specs=pl.BlockSpec((bb, n_new, tc), lambda i, j: (i, 0, j)),
        out_shape=jax.ShapeDtypeStruct(u.shape, F32),
        compiler_params=_cparams(("parallel", "parallel")),
        name="sample_conv",
    )(state, u, w_state, w_new, b_dw.reshape(1, c))


def _merge_body(a_ref, conv_ref, sgc_ref, lnw_ref, lnb_ref, woa_ref, woc_ref, ga_ref, gc_ref, o_ref, cg):
    @pl.when(pl.program_id(1) == 0)
    def _():
        x = conv_ref[...]
        mu = jnp.mean(x, axis=-1, keepdims=True)
        xc = x - mu
        y = xc * lax.rsqrt(jnp.mean(xc * xc, axis=-1, keepdims=True) + NORM_EPS)
        y = y * lnw_ref[...] + lnb_ref[...]
        cg[...] = (y * jax.nn.sigmoid(y) * sgc_ref[...].astype(F32)).astype(BF16)

    ya = jnp.dot(a_ref[...], woa_ref[...], preferred_element_type=F32)
    yc = jnp.dot(cg[...], woc_ref[...], preferred_element_type=F32)
    o_ref[...] = (ga_ref[...].astype(F32) * ya + gc_ref[...].astype(F32) * yc).astype(o_ref.dtype)


def _merge(a, conv, sgc, ln_w, ln_b, w_oa, w_oc, g_a, g_c, *, tm, tn):
    rows, d = a.shape
    row_full = pl.BlockSpec((tm, d), lambda i, j: (i, 0))
    vec = pl.BlockSpec((1, d), lambda i, j: (0, 0))
    w_spec = pl.BlockSpec((d, tn), lambda i, j: (0, j))
    tile = pl.BlockSpec((tm, tn), lambda i, j: (i, j))
    return pl.pallas_call(
        _merge_body,
        grid=(rows // tm, d // tn),
        in_specs=[row_full, row_full, row_full, vec, vec, w_spec, w_spec, tile, tile],
        out_specs=tile,
        out_shape=jax.ShapeDtypeStruct((rows, d), BF16),
        scratch_shapes=[pltpu.VMEM((tm, d), BF16)],
        compiler_params=_cparams(("parallel", "arbitrary")),
        name="branch_merge",
    )(a, conv, sgc, ln_w.reshape(1, d), ln_b.reshape(1, d), w_oa, w_oc, g_a, g_c)


def _out_body(m_ref, w_ref, h_ref, g_ref, o_ref):
    y = jnp.dot(m_ref[...], w_ref[...], preferred_element_type=F32)
    yn = y * lax.rsqrt(jnp.mean(y * y, axis=-1, keepdims=True) + NORM_EPS) * g_ref[...]
    o_ref[...] = h_ref[...] + yn


def _out_proj(merged, w_out, h, g_post, *, tm):
    rows, d = h.shape
    row = pl.BlockSpec((tm, d), lambda i: (i, 0))
    return pl.pallas_call(
        _out_body,
        grid=(rows // tm,),
        in_specs=[row, pl.BlockSpec((d, d), lambda i: (0, 0)), row, pl.BlockSpec((1, d), lambda i: (0, 0))],
        out_specs=row,
        out_shape=jax.ShapeDtypeStruct((rows, d), F32),
        compiler_params=_cparams(("parallel",)),
        name="out_proj",
    )(merged, w_out, h, g_post.reshape(1, d))


def _rope_tables128(pos):
    inv = 1.0 / (ROPE_THETA ** (jnp.arange(0, QK_ROPE, 2, dtype=F32) / QK_ROPE))
    ang = pos.astype(F32)[:, None] * inv[None, :]
    cos, sin = jnp.cos(ang), jnp.sin(ang)
    return jnp.concatenate([cos, cos, cos, cos], axis=1), jnp.concatenate([-sin, sin, -sin, sin], axis=1)


def _split_w_in(w_in):
    a = Q_COLS
    b = a + KV_RANK
    c = b + QK_ROPE
    d = c + ATTN_WIDTH
    e = d + 2 * CONV_WIDTH
    f = e + CONV_WIDTH
    w_q, w_ckv, w_kr, w_ga, w_glu, w_gc, w_mg = jnp.split(w_in, [a, b, c, d, e, f], axis=1)
    dm = w_in.shape[0]
    w_q = jnp.pad(w_q.reshape(dm, N_HEADS, QK_HEAD), ((0, 0), (0, 0), (0, QK_PAD - QK_HEAD)))
    w_q = w_q.reshape(dm, N_HEADS * QK_PAD)
    w_kr = jnp.pad(w_kr, ((0, 0), (0, LANES - QK_ROPE)))
    w_glu_a, w_glu_b = jnp.split(w_glu, 2, axis=1)
    w_mga, w_mgc = jnp.split(w_mg, 2, axis=1)
    bf = lambda w: w.astype(BF16)
    return dict(q=bf(w_q), ckv=bf(w_ckv), kr=bf(w_kr), ga=bf(w_ga), glu_a=bf(w_glu_a), glu_b=bf(w_glu_b),
                gc=bf(w_gc), mga=bf(w_mga), mgc=bf(w_mgc))


def _project_all(h, pos, w, g_pre, g_kv, *, tm):
    hn = _rmsnorm(h, g_pre, tm)
    cos_t, sin_t = _rope_tables128(pos)
    hpt = 4
    q256, = _proj(hn, [w["q"]], functools.partial(_ep_q, hpt), [(N_HEADS * QK_PAD, hpt * QK_PAD, BF16)],
                  tm=tm, tns=[hpt * QK_PAD], row_aux=(cos_t, sin_t), name="proj_q")
    ckv, kr, kr_pad = _proj(hn, [w["ckv"], w["kr"]], _ep_ckv_kr,
                            [(KV_RANK, KV_RANK, F32), (QK_ROPE, QK_ROPE, F32), (LANES, LANES, BF16)],
                            tm=tm, tns=[KV_RANK, LANES], vec_aux=(g_kv.reshape(1, KV_RANK),),
                            row_aux=(cos_t, sin_t), name="proj_ckv_kr")
    sga, = _proj(hn, [w["ga"]], _ep_silu, [(ATTN_WIDTH, 1024, BF16)], tm=tm, tns=[1024], name="proj_gate_a")
    u, = _proj(hn, [w["glu_a"], w["glu_b"]], _ep_glu, [(CONV_WIDTH, 512, F32)], tm=tm, tns=[512, 512],
               name="proj_glu")
    sgc, = _proj(hn, [w["gc"]], _ep_silu, [(CONV_WIDTH, 1024, BF16)], tm=tm, tns=[1024], name="proj_gate_c")
    g_a, g_c = _proj(hn, [w["mga"], w["mgc"]], _ep_sigmoid2, [(D_MODEL, 512, BF16), (D_MODEL, 512, BF16)],
                     tm=tm, tns=[512, 512], name="proj_merge_gates")
    return q256, ckv, kr, kr_pad, sga, u, sgc, g_a, g_c


def _finish(a, conv, sgc, g_a, g_c, h, lw, *, tm):
    merged = _merge(a, conv, sgc, lw["ln_w"], lw["ln_b"], lw["w_oa"], lw["w_oc"], g_a, g_c, tm=tm, tn=512)
    return _out_proj(merged, lw["w_out"], h, lw["g_post"], tm=tm)


def kernel(x_prompt, x_sample, cache_ckv, cache_krope, state_conv, page_table, meta_tokens, w_norm_pre,
           w_norm_post, w_in, w_kv_norm, w_uk, w_uv, w_o_attn, w_dw, b_dw, w_conv_ln, b_conv_ln, w_o_conv,
           w_out):
    batch, seq, d = x_prompt.shape
    nb, n_new, _ = x_sample.shape
    depth = w_in.shape[0]
    t_real = seq + N_META
    t_pad = -(-t_real // T_TILE) * T_TILE
    past_len = page_table.shape[1] * cache_ckv.shape[2]
    hist = CONV_K - 1

    meta = jnp.broadcast_to(meta_tokens[None].astype(x_prompt.dtype), (batch, N_META, d))
    hp = jnp.concatenate([meta, x_prompt, jnp.zeros((batch, t_pad - t_real, d), x_prompt.dtype)], axis=1)
    hp = hp.reshape(batch * t_pad, d)
    hs = x_sample.reshape(nb * n_new, d)
    pos_p = jnp.tile(jnp.arange(t_pad), batch)
    pos_s = jnp.tile(past_len + jnp.arange(n_new), nb)
    tm_p, tm_s = 768, nb * n_new

    outs = [[] for _ in range(6)]
    for l in range(depth):
        w = _split_w_in(w_in[l])
        w_uk_l = w_uk[l].reshape(KV_RANK, N_HEADS * QK_NOPE).astype(BF16)
        w_uv_l = w_uv[l].reshape(KV_RANK, N_HEADS * V_HEAD).astype(BF16)
        w_uk_t = jnp.transpose(w_uk[l], (1, 2, 0)).astype(BF16)
        w_uv_h = jnp.transpose(w_uv[l], (1, 0, 2)).astype(BF16)
        lw = dict(ln_w=w_conv_ln[l], ln_b=b_conv_ln[l], w_oa=w_o_attn[l].astype(BF16),
                  w_oc=w_o_conv[l].astype(BF16), w_out=w_out[l].astype(BF16), g_post=w_norm_post[l])

        q256, ckv_p, kr_p, kr_pad, sga, u_p, sgc, g_a, g_c = _project_all(
            hp, pos_p, w, w_norm_pre[l], w_kv_norm[l], tm=tm_p)
        k256, v = _proj(ckv_p, [w_uk_l, w_uv_l], functools.partial(_ep_kv_up, 8),
                        [(N_HEADS * QK_PAD, 8 * QK_PAD, BF16), (ATTN_WIDTH, 8 * V_HEAD, BF16)],
                        tm=tm_p, tns=[8 * QK_NOPE, 8 * V_HEAD], row_aux=(kr_pad,), name="kv_up")
        a_p = _prompt_attn(q256, k256, v, sga, batch=batch, t_pad=t_pad, tq=tm_p, tk=T_TILE)
        conv_p = _prompt_conv(u_p, w_dw[l], b_dw[l], batch=batch, t_pad=t_pad, tc=256)
        hp_new = _finish(a_p, conv_p, sgc, g_a, g_c, hp, lw, tm=512)

        q256, ckv_s, kr_s, _, sga, u_s, sgc, g_a, g_c = _project_all(
            hs, pos_s, w, w_norm_pre[l], w_kv_norm[l], tm=tm_s)
        q4 = q256.reshape(nb * n_new, N_HEADS, QK_PAD)
        q_nope = q4[:, :, :QK_NOPE].reshape(nb * n_new, N_HEADS * QK_NOPE)
        q_rope = q4[:, :, QK_NOPE:QK_HEAD].reshape(nb, n_new * N_HEADS, QK_ROPE)
        q_lat = _head_mm(q_nope, w_uk_t, out_dtype=BF16, name="q_absorb")
        q_lat = q_lat.reshape(nb, n_new * N_HEADS, KV_RANK)
        pad_new = lambda x: jnp.pad(x.reshape(nb, n_new, -1), ((0, 0), (0, NEW_PAD - n_new), (0, 0))).astype(BF16)
        o_lat = _sample_attn(page_table, q_lat, q_rope, pad_new(ckv_s), pad_new(kr_s),
                             cache_ckv[l], cache_krope[l], n_new=n_new)
        a_s = _head_mm(o_lat.reshape(nb * n_new, N_HEADS * KV_RANK), w_uv_h, sga, out_dtype=BF16, name="v_up")
        conv_s = _sample_conv(state_conv[l], u_s.reshape(nb, n_new, d), w_dw[l], b_dw[l], bb=8, tc=512)
        hs_new = _finish(a_s, conv_s.reshape(nb * n_new, d), sgc, g_a, g_c, hs, lw, tm=tm_s)

        unpad = lambda x: x.reshape(batch, t_pad, -1)[:, :t_real]
        outs[0].append(unpad(ckv_p))
        outs[1].append(unpad(kr_p))
        outs[2].append(unpad(u_p)[:, t_real - hist:])
        outs[3].append(ckv_s.reshape(nb, n_new, KV_RANK))
        outs[4].append(kr_s.reshape(nb, n_new, QK_ROPE))
        outs[5].append(jnp.concatenate([state_conv[l], u_s.reshape(nb, n_new, d)], axis=1)[:, n_new:])
        hp, hs = hp_new, hs_new

    y_prompt = hp.reshape(batch, t_pad, d)[:, N_META:t_real]
    y_sample = hs.reshape(nb, n_new, d)
    return (y_prompt, y_sample) + tuple(jnp.stack(o) for o in outs)
```

```python
import functools
import math

import jax
import jax.numpy as jnp
from jax import lax
from jax.experimental import pallas as pl
from jax.experimental.pallas import tpu as pltpu

F32 = jnp.float32
BF16 = jnp.bfloat16

D_MODEL = 2048
N_META = 16
N_HEADS = 16
QK_NOPE = 128
QK_ROPE = 64
QK_HEAD = QK_NOPE + QK_ROPE
V_HEAD = 128
KV_RANK = 512
ATTN_WIDTH = N_HEADS * V_HEAD
CONV_WIDTH = D_MODEL
CONV_K = 31
ROPE_THETA = 10000.0
NORM_EPS = 1e-6
ATTN_SCALE = 1.0 / math.sqrt(QK_HEAD)
EXP2_SCALE = ATTN_SCALE * math.log2(math.e)
Q_COLS = N_HEADS * QK_HEAD

LANES = 128
SUBLANES = 8
QK_PAD = 2 * LANES
T_TILE = 256
PAGES_PER_STEP = 16
SAMPLE_SUB_CHUNKS = 4
NEW_PAD = 16
VMEM_LIMIT = 56 * 1024 * 1024


def _cparams(sem):
    return pltpu.CompilerParams(dimension_semantics=sem, vmem_limit_bytes=VMEM_LIMIT)


def _rmsnorm_body(x_ref, g_ref, o_ref):
    x = x_ref[...]
    y = x * lax.rsqrt(jnp.mean(x * x, axis=-1, keepdims=True) + NORM_EPS)
    o_ref[...] = (y * g_ref[...]).astype(o_ref.dtype)


def _rmsnorm(x, g, tm):
    rows, d = x.shape
    return pl.pallas_call(
        _rmsnorm_body,
        grid=(rows // tm,),
        in_specs=[pl.BlockSpec((tm, d), lambda i: (i, 0)), pl.BlockSpec((1, d), lambda i: (0, 0))],
        out_specs=pl.BlockSpec((tm, d), lambda i: (i, 0)),
        out_shape=jax.ShapeDtypeStruct((rows, d), BF16),
        compiler_params=_cparams(("parallel",)),
        name="rmsnorm_pre",
    )(x, g.reshape(1, d))


def _proj_body(epilogue, n_w, n_aux, x_ref, *refs):
    w_refs, aux_refs, out_refs = refs[:n_w], refs[n_w:n_w + n_aux], refs[n_w + n_aux:]
    x = x_ref[...].astype(BF16)
    accs = [jnp.dot(x, w[...], preferred_element_type=F32) for w in w_refs]
    outs = epilogue(accs, [a[...] for a in aux_refs])
    for o_ref, o in zip(out_refs, outs):
        o_ref[...] = o.astype(o_ref.dtype)


def _proj(x, ws, epilogue, outs, *, tm, tns, vec_aux=(), row_aux=(), name):
    rows, k = x.shape
    grid = (rows // tm, ws[0].shape[1] // tns[0])
    in_specs = [pl.BlockSpec((tm, k), lambda i, j: (i, 0))]
    in_specs += [pl.BlockSpec((k, tn), lambda i, j: (0, j)) for tn in tns]
    in_specs += [pl.BlockSpec((1, a.shape[1]), lambda i, j: (0, 0)) for a in vec_aux]
    in_specs += [pl.BlockSpec((tm, a.shape[1]), lambda i, j: (i, 0)) for a in row_aux]
    out_specs = [pl.BlockSpec((tm, tw), lambda i, j: (i, j)) for (_, tw, _) in outs]
    out_shape = [jax.ShapeDtypeStruct((rows, w), dt) for (w, _, dt) in outs]
    return pl.pallas_call(
        functools.partial(_proj_body, epilogue, len(ws), len(vec_aux) + len(row_aux)),
        grid=grid, in_specs=in_specs, out_specs=out_specs, out_shape=out_shape,
        compiler_params=_cparams(("parallel", "arbitrary")),
        name=name,
    )(x, *ws, *vec_aux, *row_aux)


def _swap_rope_halves(x):
    lane = lax.broadcasted_iota(jnp.int32, x.shape, 1)
    first_half = (lane % QK_ROPE) < (QK_ROPE // 2)
    return jnp.where(first_half, pltpu.roll(x, LANES - QK_ROPE // 2, 1), pltpu.roll(x, QK_ROPE // 2, 1))


def _rope128(x, cos_t, sin_t):
    return x * cos_t + _swap_rope_halves(x) * sin_t


def _ep_q(heads_per_tile, accs, aux):
    acc, = accs
    cos_t, sin_t = aux
    parts = []
    for h in range(heads_per_tile):
        parts.append(acc[:, h * QK_PAD:h * QK_PAD + LANES])
        parts.append(_rope128(acc[:, h * QK_PAD + LANES:(h + 1) * QK_PAD], cos_t, sin_t))
    return [jnp.concatenate(parts, axis=1)]


def _ep_ckv_kr(accs, aux):
    c, kr = accs
    g, cos_t, sin_t = aux
    ckv = c * lax.rsqrt(jnp.mean(c * c, axis=-1, keepdims=True) + NORM_EPS) * g
    kr = _rope128(kr, cos_t, sin_t)
    return [ckv, kr[:, :QK_ROPE], kr]


def _ep_silu(accs, aux):
    a, = accs
    return [a * jax.nn.sigmoid(a)]


def _ep_glu(accs, aux):
    a, b = accs
    return [a * jax.nn.sigmoid(b)]


def _ep_sigmoid2(accs, aux):
    a, b = accs
    return [jax.nn.sigmoid(a), jax.nn.sigmoid(b)]


def _ep_kv_up(heads_per_tile, accs, aux):
    kn, v = accs
    kr_pad, = aux
    parts = []
    for h in range(heads_per_tile):
        parts.append(kn[:, h * QK_NOPE:(h + 1) * QK_NOPE].astype(BF16))
        parts.append(kr_pad)
    return [jnp.concatenate(parts, axis=1), v]


def _head_mm_body(x_ref, w_ref, o_ref):
    o_ref[...] = jnp.dot(x_ref[...].astype(BF16), w_ref[0], preferred_element_type=F32).astype(o_ref.dtype)


def _head_mm_gated_body(x_ref, w_ref, g_ref, o_ref):
    y = jnp.dot(x_ref[...].astype(BF16), w_ref[0], preferred_element_type=F32)
    o_ref[...] = (y * g_ref[...].astype(F32)).astype(o_ref.dtype)


def _head_mm(x, w, gate=None, *, out_dtype, name):
    rows = x.shape[0]
    nh, k, n = w.shape
    in_specs = [pl.BlockSpec((rows, k), lambda h: (0, h)), pl.BlockSpec((1, k, n), lambda h: (h, 0, 0))]
    args = [x, w]
    body = _head_mm_body
    if gate is not None:
        in_specs.append(pl.BlockSpec((rows, n), lambda h: (0, h)))
        args.append(gate)
        body = _head_mm_gated_body
    return pl.pallas_call(
        body, grid=(nh,), in_specs=in_specs,
        out_specs=pl.BlockSpec((rows, n), lambda h: (0, h)),
        out_shape=jax.ShapeDtypeStruct((rows, nh * n), out_dtype),
        compiler_params=_cparams(("parallel",)),
        name=name,
    )(*args)


def _lane_tile(x, width):
    return jnp.concatenate([x] * (width // LANES), axis=1)


def _prompt_attn_body(q_ref, k_ref, v_ref, g_ref, o_ref, m_sc, l_sc, acc_sc, *, tq, tk):
    qi = pl.program_id(2)
    m_sc[...] = jnp.full(m_sc.shape, -jnp.inf, F32)
    l_sc[...] = jnp.zeros(l_sc.shape, F32)
    acc_sc[...] = jnp.zeros(acc_sc.shape, F32)

    def tile(ki, row0, masked):
        rows = pl.ds(row0, tq - row0)
        ks = pl.multiple_of(ki * tk, tk)
        k = k_ref[pl.ds(ks, tk), :]
        v = v_ref[pl.ds(ks, tk), :]
        s = lax.dot_general(q_ref[rows, :], k, (((1,), (1,)), ((), ())), preferred_element_type=F32)
        if masked:
            q_pos = qi * tq + row0 + lax.broadcasted_iota(jnp.int32, s.shape, 0)
            k_pos = ks + lax.broadcasted_iota(jnp.int32, s.shape, 1)
            s = jnp.where(k_pos <= q_pos, s, -jnp.inf)
        m_old = m_sc[rows, :]
        m_new = jnp.maximum(m_old, jnp.max(s, axis=-1, keepdims=True))
        alpha = jnp.exp2((m_old - m_new) * EXP2_SCALE)
        p = jnp.exp2((s - _lane_tile(m_new, tk)) * EXP2_SCALE)
        l_sc[rows, :] = alpha * l_sc[rows, :] + jnp.sum(p, axis=-1, keepdims=True)
        acc_sc[rows, :] = alpha * acc_sc[rows, :] + jnp.dot(p.astype(BF16), v, preferred_element_type=F32)
        m_sc[rows, :] = m_new

    n_full = qi * (tq // tk)

    def full_tile(ki, carry):
        tile(ki, 0, False)
        return carry

    lax.fori_loop(0, n_full, full_tile, 0)
    for d in range(tq // tk):
        tile(n_full + d, d * tk, True)
    o_ref[...] = (acc_sc[...] / l_sc[...] * g_ref[...].astype(F32)).astype(o_ref.dtype)


def _prompt_attn(q256, k256, v, gate, *, batch, t_pad, tq, tk):
    nq = t_pad // tq
    return pl.pallas_call(
        functools.partial(_prompt_attn_body, tq=tq, tk=tk),
        grid=(batch, N_HEADS, nq),
        in_specs=[
            pl.BlockSpec((tq, QK_PAD), lambda b, h, i: (b * nq + i, h)),
            pl.BlockSpec((t_pad, QK_PAD), lambda b, h, i: (b, h)),
            pl.BlockSpec((t_pad, V_HEAD), lambda b, h, i: (b, h)),
            pl.BlockSpec((tq, V_HEAD), lambda b, h, i: (b * nq + i, h)),
        ],
        out_specs=pl.BlockSpec((tq, V_HEAD), lambda b, h, i: (b * nq + i, h)),
        out_shape=jax.ShapeDtypeStruct((batch * t_pad, ATTN_WIDTH), BF16),
        scratch_shapes=[pltpu.VMEM((tq, LANES), F32), pltpu.VMEM((tq, LANES), F32),
                        pltpu.VMEM((tq, V_HEAD), F32)],
        compiler_params=_cparams(("parallel", "parallel", "arbitrary")),
        name="prompt_attn",
    )(q256, k256, v, gate)


def _sample_attn_body(pt_ref, ql_ref, qr_ref, cn_ref, kn_ref, c_hbm, rt_hbm, o_ref,
                      c_stage, rt_stage, c_sem, rt_sem, *scratch, layer, n_pages, n_sub):
    cbufs, rtbufs = scratch[:n_sub], scratch[n_sub:2 * n_sub]
    m_sc, l_sc, acc_sc = scratch[2 * n_sub:]
    req = pl.program_id(0)
    step = pl.program_id(1)
    n_steps = pl.num_programs(1)
    flat = req * n_steps + step
    slot = flat % 2

    def page_copies(r, s, sl, lookup):
        copies = []
        for k in range(n_pages):
            pg = pt_ref[r, s * n_pages + k] if lookup else 0
            copies.append(pltpu.make_async_copy(c_hbm.at[layer, pg], c_stage.at[sl, k], c_sem.at[sl]))
            copies.append(pltpu.make_async_copy(rt_hbm.at[layer, pg], rt_stage.at[sl, k], rt_sem.at[sl]))
        return copies

    @pl.when(flat == 0)
    def _():
        for cp in page_copies(req, step, slot, True):
            cp.start()

    ql = ql_ref[0]
    qr = qr_ref[0]
    dn_t = (((1,), (1,)), ((), ()))

    @pl.when(step == 0)
    def _():
        cn = cn_ref[0]
        kn = kn_ref[0]
        s = (lax.dot_general(ql, cn, dn_t, preferred_element_type=F32)
             + lax.dot_general(qr, kn, dn_t, preferred_element_type=F32))
        tok = lax.broadcasted_iota(jnp.int32, s.shape, 0) // N_HEADS
        key = lax.broadcasted_iota(jnp.int32, s.shape, 1)
        s = jnp.where(key <= tok, s, -jnp.inf)
        m = jnp.max(s, axis=-1, keepdims=True)
        p = jnp.exp2((s - m) * EXP2_SCALE)
        m_sc[...] = jnp.broadcast_to(m, m_sc.shape)
        l_sc[...] = jnp.broadcast_to(jnp.sum(p, axis=-1, keepdims=True), l_sc.shape)
        acc_sc[...] = jnp.dot(p.astype(BF16), cn, preferred_element_type=F32)

    last_step = step + 1 == n_steps
    next_req = jnp.where(last_step, jnp.where(req + 1 == pl.num_programs(0), 0, req + 1), req)
    next_step = jnp.where(last_step, 0, step + 1)
    for cp in page_copies(next_req, next_step, 1 - slot, True):
        cp.start()
    for cp in page_copies(req, step, slot, False):
        cp.wait()

    per = n_pages // n_sub
    page = c_stage.shape[2]
    scores = []
    for sub in range(n_sub):
        for i in range(per):
            k = sub * per + i
            cbufs[sub][i * page:(i + 1) * page, :] = c_stage[slot, k].astype(BF16)
            rtbufs[sub][:, i * page:(i + 1) * page] = rt_stage[slot, k].astype(BF16)
        scores.append(lax.dot_general(ql, cbufs[sub][...], dn_t, preferred_element_type=F32)
                      + jnp.dot(qr, rtbufs[sub][...], preferred_element_type=F32))
    m_run, l_run, acc = m_sc[...], l_sc[...], acc_sc[...]
    for sub in range(n_sub):
        s = scores[sub]
        m_new = jnp.maximum(m_run, jnp.max(s, axis=-1, keepdims=True))
        alpha = jnp.exp2((m_run - m_new) * EXP2_SCALE)
        p = jnp.exp2((s - _lane_tile(m_new, per * page)) * EXP2_SCALE)
        l_run = alpha * l_run + jnp.sum(p, axis=-1, keepdims=True)
        acc = (_lane_tile(alpha, KV_RANK) * acc
               + jnp.dot(p.astype(BF16), cbufs[sub][...], preferred_element_type=F32))
        m_run = m_new
    m_sc[...], l_sc[...], acc_sc[...] = m_run, l_run, acc

    @pl.when(last_step)
    def _():
        o_ref[0] = (acc / _lane_tile(l_run, KV_RANK)).astype(o_ref.dtype)

    @pl.when(flat + 1 == pl.num_programs(0) * n_steps)
    def _():
        for cp in page_copies(next_req, next_step, 1 - slot, False):
            cp.wait()


def _sample_attn(page_table, q_lat, q_rope, cn_pad, kn_pad, cache_c, cache_rt, *, layer):
    nb, n_rows, _ = q_lat.shape
    n_tbl = page_table.shape[1]
    npg = PAGES_PER_STEP
    n_steps = n_tbl // npg
    page = cache_c.shape[2]

    def q_map(b, s, pt):
        return (b, 0, 0)

    in_specs = [
        pl.BlockSpec((1, n_rows, KV_RANK), q_map),
        pl.BlockSpec((1, n_rows, QK_ROPE), q_map),
        pl.BlockSpec((1, NEW_PAD, KV_RANK), q_map),
        pl.BlockSpec((1, NEW_PAD, QK_ROPE), q_map),
        pl.BlockSpec(memory_space=pl.ANY),
        pl.BlockSpec(memory_space=pl.ANY),
    ]
    grid_spec = pltpu.PrefetchScalarGridSpec(
        num_scalar_prefetch=1,
        grid=(nb, n_steps),
        in_specs=in_specs,
        out_specs=pl.BlockSpec((1, n_rows, KV_RANK), q_map),
        scratch_shapes=[
            pltpu.VMEM((2, npg, page, KV_RANK), F32),
            pltpu.VMEM((2, npg, QK_ROPE, page), F32),
            pltpu.SemaphoreType.DMA((2,)),
            pltpu.SemaphoreType.DMA((2,)),
            *[pltpu.VMEM((npg // SAMPLE_SUB_CHUNKS * page, KV_RANK), BF16)] * SAMPLE_SUB_CHUNKS,
            *[pltpu.VMEM((QK_ROPE, npg // SAMPLE_SUB_CHUNKS * page), BF16)] * SAMPLE_SUB_CHUNKS,
            pltpu.VMEM((n_rows, LANES), F32),
            pltpu.VMEM((n_rows, LANES), F32),
            pltpu.VMEM((n_rows, KV_RANK), F32),
        ],
    )
    return pl.pallas_call(
        functools.partial(_sample_attn_body, layer=layer, n_pages=npg, n_sub=SAMPLE_SUB_CHUNKS),
        grid_spec=grid_spec,
        out_shape=jax.ShapeDtypeStruct((nb, n_rows, KV_RANK), F32),
        compiler_params=_cparams(("arbitrary", "arbitrary")),
        name="sample_attn",
    )(page_table, q_lat, q_rope, cn_pad, kn_pad, cache_c, cache_rt)


CONV_HALO = 32
CONV_CHUNK = 128


def _prompt_conv_body(u_ref, w_ref, b_ref, o_ref, ext, *, t_pad):
    tc = u_ref.shape[1]
    ext[0:CONV_HALO, :] = jnp.zeros((CONV_HALO, tc), F32)
    ext[CONV_HALO:CONV_HALO + t_pad, :] = u_ref[...]
    ext[CONV_HALO + t_pad:, :] = jnp.zeros((SUBLANES, tc), F32)
    off = CONV_HALO - (CONV_K - 1)
    span = CONV_CHUNK + SUBLANES
    for c in range(t_pad // CONV_CHUNK):
        base = c * CONV_CHUNK
        acc = jnp.broadcast_to(b_ref[...], (CONV_CHUNK, tc))
        for r in range(SUBLANES):
            z = None
            for k in range(CONV_K):
                if (off + k) % SUBLANES != r:
                    continue
                a = base + off + k - r
                term = ext[a:a + span, :] * w_ref[k:k + 1, :]
                z = term if z is None else z + term
            acc = acc + z[r:r + CONV_CHUNK, :]
        o_ref[base:base + CONV_CHUNK, :] = acc


def _prompt_conv(u, w_dw, b_dw, *, batch, t_pad, tc):
    c = u.shape[1]
    return pl.pallas_call(
        functools.partial(_prompt_conv_body, t_pad=t_pad),
        grid=(batch, c // tc),
        in_specs=[
            pl.BlockSpec((t_pad, tc), lambda b, j: (b, j)),
            pl.BlockSpec((CONV_K, tc), lambda b, j: (0, j)),
            pl.BlockSpec((1, tc), lambda b, j: (0, j)),
        ],
        out_specs=pl.BlockSpec((t_pad, tc), lambda b, j: (b, j)),
        out_shape=jax.ShapeDtypeStruct(u.shape, F32),
        scratch_shapes=[pltpu.VMEM((CONV_HALO + t_pad + SUBLANES, tc), F32)],
        compiler_params=_cparams(("parallel", "parallel")),
        name="prompt_conv",
    )(u, w_dw, b_dw.reshape(1, c))


def _sample_conv_body(st_ref, u_ref, ws_ref, wu_ref, b_ref, o_ref, *, n_new):
    st = st_ref[...]
    u = u_ref[...]
    for t in range(n_new):
        y = jnp.sum(st * ws_ref[t][None], axis=1) + jnp.sum(u * wu_ref[t][None], axis=1)
        o_ref[:, t, :] = y + b_ref[...]


def _sample_conv(state, u, w_dw, b_dw, *, bb, tc):
    nb, hist, c = state.shape
    n_new = u.shape[1]
    w_ext = jnp.stack([jnp.pad(w_dw, ((t, n_new - 1 - t), (0, 0))) for t in range(n_new)])
    w_state = w_ext[:, :hist]
    w_new = w_ext[:, hist:]
    return pl.pallas_call(
        functools.partial(_sample_conv_body, n_new=n_new),
        grid=(nb // bb, c // tc),
        in_specs=[
            pl.BlockSpec((bb, hist, tc), lambda i, j: (i, 0, j)),
            pl.BlockSpec((bb, n_new, tc), lambda i, j: (i, 0, j)),
            pl.BlockSpec((n_new, hist, tc), lambda i, j: (0, 0, j)),
            pl.BlockSpec((n_new, n_new, tc), lambda i, j: (0, 0, j)),
            pl.BlockSpec((1, tc), lambda i, j: (0, j)),
        ],
        out_specs=pl.BlockSpec((bb, n_new, tc), lambda i, j: (i, 0, j)),
        out_shape=jax.ShapeDtypeStruct(u.shape, F32),
        compiler_params=_cparams(("parallel", "parallel")),
        name="sample_conv",
    )(state, u, w_state, w_new, b_dw.reshape(1, c))


def _merge_body(a_ref, conv_ref, sgc_ref, lnw_ref, lnb_ref, woa_ref, woc_ref, ga_ref, gc_ref, o_ref, cg):
    @pl.when(pl.program_id(1) == 0)
    def _():
        x = conv_ref[...]
        mu = jnp.mean(x, axis=-1, keepdims=True)
        xc = x - mu
        y = xc * lax.rsqrt(jnp.mean(xc * xc, axis=-1, keepdims=True) + NORM_EPS)
        y = y * lnw_ref[...] + lnb_ref[...]
        cg[...] = (y * jax.nn.sigmoid(y) * sgc_ref[...].astype(F32)).astype(BF16)

    ya = jnp.dot(a_ref[...], woa_ref[...], preferred_element_type=F32)
    yc = jnp.dot(cg[...], woc_ref[...], preferred_element_type=F32)
    o_ref[...] = (ga_ref[...].astype(F32) * ya + gc_ref[...].astype(F32) * yc).astype(o_ref.dtype)


def _merge(a, conv, sgc, ln_w, ln_b, w_oa, w_oc, g_a, g_c, *, tm, tn):
    rows, d = a.shape
    row_full = pl.BlockSpec((tm, d), lambda i, j: (i, 0))
    vec = pl.BlockSpec((1, d), lambda i, j: (0, 0))
    w_spec = pl.BlockSpec((d, tn), lambda i, j: (0, j))
    tile = pl.BlockSpec((tm, tn), lambda i, j: (i, j))
    return pl.pallas_call(
        _merge_body,
        grid=(rows // tm, d // tn),
        in_specs=[row_full, row_full, row_full, vec, vec, w_spec, w_spec, tile, tile],
        out_specs=tile,
        out_shape=jax.ShapeDtypeStruct((rows, d), BF16),
        scratch_shapes=[pltpu.VMEM((tm, d), BF16)],
        compiler_params=_cparams(("parallel", "arbitrary")),
        name="branch_merge",
    )(a, conv, sgc, ln_w.reshape(1, d), ln_b.reshape(1, d), w_oa, w_oc, g_a, g_c)


def _out_body(m_ref, w_ref, h_ref, g_ref, o_ref):
    y = jnp.dot(m_ref[...], w_ref[...], preferred_element_type=F32)
    yn = y * lax.rsqrt(jnp.mean(y * y, axis=-1, keepdims=True) + NORM_EPS) * g_ref[...]
    o_ref[...] = h_ref[...] + yn


def _out_proj(merged, w_out, h, g_post, *, tm):
    rows, d = h.shape
    row = pl.BlockSpec((tm, d), lambda i: (i, 0))
    return pl.pallas_call(
        _out_body,
        grid=(rows // tm,),
        in_specs=[row, pl.BlockSpec((d, d), lambda i: (0, 0)), row, pl.BlockSpec((1, d), lambda i: (0, 0))],
        out_specs=row,
        out_shape=jax.ShapeDtypeStruct((rows, d), F32),
        compiler_params=_cparams(("parallel",)),
        name="out_proj",
    )(merged, w_out, h, g_post.reshape(1, d))


def _rope_tables128(pos):
    inv = 1.0 / (ROPE_THETA ** (jnp.arange(0, QK_ROPE, 2, dtype=F32) / QK_ROPE))
    ang = pos.astype(F32)[:, None] * inv[None, :]
    cos, sin = jnp.cos(ang), jnp.sin(ang)
    return jnp.concatenate([cos, cos, cos, cos], axis=1), jnp.concatenate([-sin, sin, -sin, sin], axis=1)


def _split_w_in(w_in):
    a = Q_COLS
    b = a + KV_RANK
    c = b + QK_ROPE
    d = c + ATTN_WIDTH
    e = d + 2 * CONV_WIDTH
    f = e + CONV_WIDTH
    w_q, w_ckv, w_kr, w_ga, w_glu, w_gc, w_mg = jnp.split(w_in, [a, b, c, d, e, f], axis=1)
    dm = w_in.shape[0]
    w_q = jnp.pad(w_q.reshape(dm, N_HEADS, QK_HEAD), ((0, 0), (0, 0), (0, QK_PAD - QK_HEAD)))
    w_q = w_q.reshape(dm, N_HEADS * QK_PAD)
    w_kr = jnp.pad(w_kr, ((0, 0), (0, LANES - QK_ROPE)))
    w_glu_a, w_glu_b = jnp.split(w_glu, 2, axis=1)
    w_mga, w_mgc = jnp.split(w_mg, 2, axis=1)
    bf = lambda w: w.astype(BF16)
    return dict(q=bf(w_q), ckv=bf(w_ckv), kr=bf(w_kr), ga=bf(w_ga), glu_a=bf(w_glu_a), glu_b=bf(w_glu_b),
                gc=bf(w_gc), mga=bf(w_mga), mgc=bf(w_mgc))


def _project_all(h, pos, w, g_pre, g_kv, *, tm):
    hn = _rmsnorm(h, g_pre, tm)
    cos_t, sin_t = _rope_tables128(pos)
    hpt = 4
    q256, = _proj(hn, [w["q"]], functools.partial(_ep_q, hpt), [(N_HEADS * QK_PAD, hpt * QK_PAD, BF16)],
                  tm=tm, tns=[hpt * QK_PAD], row_aux=(cos_t, sin_t), name="proj_q")
    ckv, kr, kr_pad = _proj(hn, [w["ckv"], w["kr"]], _ep_ckv_kr,
                            [(KV_RANK, KV_RANK, F32), (QK_ROPE, QK_ROPE, F32), (LANES, LANES, BF16)],
                            tm=tm, tns=[KV_RANK, LANES], vec_aux=(g_kv.reshape(1, KV_RANK),),
                            row_aux=(cos_t, sin_t), name="proj_ckv_kr")
    sga, = _proj(hn, [w["ga"]], _ep_silu, [(ATTN_WIDTH, 1024, BF16)], tm=tm, tns=[1024], name="proj_gate_a")
    u, = _proj(hn, [w["glu_a"], w["glu_b"]], _ep_glu, [(CONV_WIDTH, 512, F32)], tm=tm, tns=[512, 512],
               name="proj_glu")
    sgc, = _proj(hn, [w["gc"]], _ep_silu, [(CONV_WIDTH, 1024, BF16)], tm=tm, tns=[1024], name="proj_gate_c")
    g_a, g_c = _proj(hn, [w["mga"], w["mgc"]], _ep_sigmoid2, [(D_MODEL, 512, BF16), (D_MODEL, 512, BF16)],
                     tm=tm, tns=[512, 512], name="proj_merge_gates")
    return q256, ckv, kr, kr_pad, sga, u, sgc, g_a, g_c


def _finish(a, conv, sgc, g_a, g_c, h, lw, *, tm):
    merged = _merge(a, conv, sgc, lw["ln_w"], lw["ln_b"], lw["w_oa"], lw["w_oc"], g_a, g_c, tm=tm, tn=512)
    return _out_proj(merged, lw["w_out"], h, lw["g_post"], tm=tm)


def kernel(x_prompt, x_sample, cache_ckv, cache_krope, state_conv, page_table, meta_tokens, w_norm_pre,
           w_norm_post, w_in, w_kv_norm, w_uk, w_uv, w_o_attn, w_dw, b_dw, w_conv_ln, b_conv_ln, w_o_conv,
           w_out):
    batch, seq, d = x_prompt.shape
    nb, n_new, _ = x_sample.shape
    depth = w_in.shape[0]
    t_real = seq + N_META
    t_pad = -(-t_real // T_TILE) * T_TILE
    past_len = page_table.shape[1] * cache_ckv.shape[2]
    hist = CONV_K - 1

    meta = jnp.broadcast_to(meta_tokens[None].astype(x_prompt.dtype), (batch, N_META, d))
    hp = jnp.concatenate([meta, x_prompt, jnp.zeros((batch, t_pad - t_real, d), x_prompt.dtype)], axis=1)
    hp = hp.reshape(batch * t_pad, d)
    hs = x_sample.reshape(nb * n_new, d)
    pos_p = jnp.tile(jnp.arange(t_pad), batch)
    pos_s = jnp.tile(past_len + jnp.arange(n_new), nb)
    tm_p, tm_s = 768, nb * n_new
    cache_krope_t = jnp.swapaxes(cache_krope, 2, 3)

    outs = [[] for _ in range(6)]
    for l in range(depth):
        w = _split_w_in(w_in[l])
        w_uk_l = w_uk[l].reshape(KV_RANK, N_HEADS * QK_NOPE).astype(BF16)
        w_uv_l = w_uv[l].reshape(KV_RANK, N_HEADS * V_HEAD).astype(BF16)
        w_uk_t = jnp.transpose(w_uk[l], (1, 2, 0)).astype(BF16)
        w_uv_h = jnp.transpose(w_uv[l], (1, 0, 2)).astype(BF16)
        lw = dict(ln_w=w_conv_ln[l], ln_b=b_conv_ln[l], w_oa=w_o_attn[l].astype(BF16),
                  w_oc=w_o_conv[l].astype(BF16), w_out=w_out[l].astype(BF16), g_post=w_norm_post[l])

        q256, ckv_p, kr_p, kr_pad, sga, u_p, sgc, g_a, g_c = _project_all(
            hp, pos_p, w, w_norm_pre[l], w_kv_norm[l], tm=tm_p)
        k256, v = _proj(ckv_p, [w_uk_l, w_uv_l], functools.partial(_ep_kv_up, 8),
                        [(N_HEADS * QK_PAD, 8 * QK_PAD, BF16), (ATTN_WIDTH, 8 * V_HEAD, BF16)],
                        tm=tm_p, tns=[8 * QK_NOPE, 8 * V_HEAD], row_aux=(kr_pad,), name="kv_up")
        a_p = _prompt_attn(q256, k256, v, sga, batch=batch, t_pad=t_pad, tq=tm_p, tk=T_TILE)
        conv_p = _prompt_conv(u_p, w_dw[l], b_dw[l], batch=batch, t_pad=t_pad, tc=LANES)
        hp_new = _finish(a_p, conv_p, sgc, g_a, g_c, hp, lw, tm=512)

        q256, ckv_s, kr_s, _, sga, u_s, sgc, g_a, g_c = _project_all(
            hs, pos_s, w, w_norm_pre[l], w_kv_norm[l], tm=tm_s)
        q4 = q256.reshape(nb * n_new, N_HEADS, QK_PAD)
        q_nope = q4[:, :, :QK_NOPE].reshape(nb * n_new, N_HEADS * QK_NOPE)
        q_rope = q4[:, :, QK_NOPE:QK_HEAD].reshape(nb, n_new * N_HEADS, QK_ROPE)
        q_lat = _head_mm(q_nope, w_uk_t, out_dtype=BF16, name="q_absorb")
        q_lat = q_lat.reshape(nb, n_new * N_HEADS, KV_RANK)
        pad_new = lambda x: jnp.pad(x.reshape(nb, n_new, -1), ((0, 0), (0, NEW_PAD - n_new), (0, 0))).astype(BF16)
        o_lat = _sample_attn(page_table, q_lat, q_rope, pad_new(ckv_s), pad_new(kr_s),
                             cache_ckv, cache_krope_t, layer=l)
        a_s = _head_mm(o_lat.reshape(nb * n_new, N_HEADS * KV_RANK), w_uv_h, sga, out_dtype=BF16, name="v_up")
        conv_s = _sample_conv(state_conv[l], u_s.reshape(nb, n_new, d), w_dw[l], b_dw[l], bb=8, tc=512)
        hs_new = _finish(a_s, conv_s.reshape(nb * n_new, d), sgc, g_a, g_c, hs, lw, tm=tm_s)

        unpad = lambda x: x.reshape(batch, t_pad, -1)[:, :t_real]
        outs[0].append(unpad(ckv_p))
        outs[1].append(unpad(kr_p))
        outs[2].append(unpad(u_p)[:, t_real - hist:])
        outs[3].append(ckv_s.reshape(nb, n_new, KV_RANK))
        outs[4].append(kr_s.reshape(nb, n_new, QK_ROPE))
        outs[5].append(jnp.concatenate([state_conv[l], u_s.reshape(nb, n_new, d)], axis=1)[:, n_new:])
        hp, hs = hp_new, hs_new

    y_prompt = hp.reshape(batch, t_pad, d)[:, N_META:t_real]
    y_sample = hs.reshape(nb, n_new, d)
    return (y_prompt, y_sample) + tuple(jnp.stack(o) for o in outs)
```

```python
import functools
import math

import jax
import jax.numpy as jnp
from jax import lax
from jax.experimental import pallas as pl
from jax.experimental.pallas import tpu as pltpu

F32 = jnp.float32
BF16 = jnp.bfloat16

D_MODEL = 2048
N_META = 16
N_HEADS = 16
QK_NOPE = 128
QK_ROPE = 64
QK_HEAD = QK_NOPE + QK_ROPE
V_HEAD = 128
KV_RANK = 512
ATTN_WIDTH = N_HEADS * V_HEAD
CONV_WIDTH = D_MODEL
CONV_K = 31
ROPE_THETA = 10000.0
NORM_EPS = 1e-6
ATTN_SCALE = 1.0 / math.sqrt(QK_HEAD)
EXP2_SCALE = ATTN_SCALE * math.log2(math.e)
Q_COLS = N_HEADS * QK_HEAD

LANES = 128
SUBLANES = 8
QK_PAD = 2 * LANES
T_TILE = 256
PAGES_PER_STEP = 16
SAMPLE_STAGE_SLOTS = 3
SAMPLE_SUB_CHUNKS = 4
NEW_PAD = 16
VMEM_LIMIT = 56 * 1024 * 1024


def _cparams(sem):
    return pltpu.CompilerParams(dimension_semantics=sem, vmem_limit_bytes=VMEM_LIMIT)


def _rmsnorm_body(x_ref, g_ref, o_ref):
    x = x_ref[...]
    y = x * lax.rsqrt(jnp.mean(x * x, axis=-1, keepdims=True) + NORM_EPS)
    o_ref[...] = (y * g_ref[...]).astype(o_ref.dtype)


def _rmsnorm(x, g, tm):
    rows, d = x.shape
    return pl.pallas_call(
        _rmsnorm_body,
        grid=(rows // tm,),
        in_specs=[pl.BlockSpec((tm, d), lambda i: (i, 0)), pl.BlockSpec((1, d), lambda i: (0, 0))],
        out_specs=pl.BlockSpec((tm, d), lambda i: (i, 0)),
        out_shape=jax.ShapeDtypeStruct((rows, d), BF16),
        compiler_params=_cparams(("parallel",)),
        name="rmsnorm_pre",
    )(x, g.reshape(1, d))


def _proj_body(epilogue, n_w, n_aux, x_ref, *refs):
    w_refs, aux_refs, out_refs = refs[:n_w], refs[n_w:n_w + n_aux], refs[n_w + n_aux:]
    x = x_ref[...].astype(BF16)
    accs = [jnp.dot(x, w[...], preferred_element_type=F32) for w in w_refs]
    outs = epilogue(accs, [a[...] for a in aux_refs])
    for o_ref, o in zip(out_refs, outs):
        o_ref[...] = o.astype(o_ref.dtype)


def _proj(x, ws, epilogue, outs, *, tm, tns, vec_aux=(), row_aux=(), name):
    rows, k = x.shape
    grid = (rows // tm, ws[0].shape[1] // tns[0])
    in_specs = [pl.BlockSpec((tm, k), lambda i, j: (i, 0))]
    in_specs += [pl.BlockSpec((k, tn), lambda i, j: (0, j)) for tn in tns]
    in_specs += [pl.BlockSpec((1, a.shape[1]), lambda i, j: (0, 0)) for a in vec_aux]
    in_specs += [pl.BlockSpec((tm, a.shape[1]), functools.partial(lambda n, i, j: (i % n, 0), a.shape[0] // tm))
                 for a in row_aux]
    out_specs = [pl.BlockSpec((tm, tw), lambda i, j: (i, j)) for (_, tw, _) in outs]
    out_shape = [jax.ShapeDtypeStruct((rows, w), dt) for (w, _, dt) in outs]
    return pl.pallas_call(
        functools.partial(_proj_body, epilogue, len(ws), len(vec_aux) + len(row_aux)),
        grid=grid, in_specs=in_specs, out_specs=out_specs, out_shape=out_shape,
        compiler_params=_cparams(("parallel", "arbitrary")),
        name=name,
    )(x, *ws, *vec_aux, *row_aux)


def _swap_rope_halves(x):
    lane = lax.broadcasted_iota(jnp.int32, x.shape, 1)
    first_half = (lane % QK_ROPE) < (QK_ROPE // 2)
    return jnp.where(first_half, pltpu.roll(x, LANES - QK_ROPE // 2, 1), pltpu.roll(x, QK_ROPE // 2, 1))


def _rope128(x, cos_t, sin_t):
    return x * cos_t + _swap_rope_halves(x) * sin_t


def _ep_q(heads_per_tile, accs, aux):
    acc, = accs
    cos_t, sin_t = aux
    parts = []
    for h in range(heads_per_tile):
        parts.append(acc[:, h * QK_PAD:h * QK_PAD + LANES])
        parts.append(_rope128(acc[:, h * QK_PAD + LANES:(h + 1) * QK_PAD], cos_t, sin_t))
    return [jnp.concatenate(parts, axis=1)]


def _ep_ckv_kr(accs, aux):
    c, kr = accs
    g, cos_t, sin_t = aux
    ckv = c * lax.rsqrt(jnp.mean(c * c, axis=-1, keepdims=True) + NORM_EPS) * g
    kr = _rope128(kr, cos_t, sin_t)
    return [ckv, kr[:, :QK_ROPE], kr]


def _ep_silu(accs, aux):
    a, = accs
    return [a * jax.nn.sigmoid(a)]


def _ep_glu(accs, aux):
    a, b = accs
    return [a * jax.nn.sigmoid(b)]


def _ep_sigmoid2(accs, aux):
    a, b = accs
    return [jax.nn.sigmoid(a), jax.nn.sigmoid(b)]


def _ep_kv_up(heads_per_tile, accs, aux):
    kn, v = accs
    kr_pad, = aux
    parts = []
    for h in range(heads_per_tile):
        parts.append(kn[:, h * QK_NOPE:(h + 1) * QK_NOPE].astype(BF16))
        parts.append(kr_pad)
    return [jnp.concatenate(parts, axis=1), v]


def _head_mm_body(x_ref, w_ref, o_ref):
    o_ref[...] = jnp.dot(x_ref[...].astype(BF16), w_ref[0], preferred_element_type=F32).astype(o_ref.dtype)


def _head_mm_gated_body(x_ref, w_ref, g_ref, o_ref):
    y = jnp.dot(x_ref[...].astype(BF16), w_ref[0], preferred_element_type=F32)
    o_ref[...] = (y * g_ref[...].astype(F32)).astype(o_ref.dtype)


def _head_mm(x, w, gate=None, *, out_dtype, name):
    rows = x.shape[0]
    nh, k, n = w.shape
    in_specs = [pl.BlockSpec((rows, k), lambda h: (0, h)), pl.BlockSpec((1, k, n), lambda h: (h, 0, 0))]
    args = [x, w]
    body = _head_mm_body
    if gate is not None:
        in_specs.append(pl.BlockSpec((rows, n), lambda h: (0, h)))
        args.append(gate)
        body = _head_mm_gated_body
    return pl.pallas_call(
        body, grid=(nh,), in_specs=in_specs,
        out_specs=pl.BlockSpec((rows, n), lambda h: (0, h)),
        out_shape=jax.ShapeDtypeStruct((rows, nh * n), out_dtype),
        compiler_params=_cparams(("parallel",)),
        name=name,
    )(*args)


def _lane_tile(x, width):
    return jnp.concatenate([x] * (width // LANES), axis=1)


def _prompt_attn_body(q_ref, k_ref, v_ref, g_ref, o_ref, m_sc, l_sc, acc_sc, *, tq, tk):
    qi = pl.program_id(2)
    m_sc[...] = jnp.full(m_sc.shape, -jnp.inf, F32)
    l_sc[...] = jnp.zeros(l_sc.shape, F32)
    acc_sc[...] = jnp.zeros(acc_sc.shape, F32)

    def tile(ki, row0, masked):
        rows = pl.ds(row0, tq - row0)
        ks = pl.multiple_of(ki * tk, tk)
        k = k_ref[pl.ds(ks, tk), :]
        v = v_ref[pl.ds(ks, tk), :]
        s = lax.dot_general(q_ref[rows, :], k, (((1,), (1,)), ((), ())), preferred_element_type=F32)
        if masked:
            q_pos = qi * tq + row0 + lax.broadcasted_iota(jnp.int32, s.shape, 0)
            k_pos = ks + lax.broadcasted_iota(jnp.int32, s.shape, 1)
            s = jnp.where(k_pos <= q_pos, s, -jnp.inf)
        m_old = m_sc[rows, :]
        m_new = jnp.maximum(m_old, jnp.max(s, axis=-1, keepdims=True))
        alpha = jnp.exp2((m_old - m_new) * EXP2_SCALE)
        p = jnp.exp2((s - _lane_tile(m_new, tk)) * EXP2_SCALE)
        l_sc[rows, :] = alpha * l_sc[rows, :] + jnp.sum(p, axis=-1, keepdims=True)
        acc_sc[rows, :] = alpha * acc_sc[rows, :] + jnp.dot(p.astype(BF16), v, preferred_element_type=F32)
        m_sc[rows, :] = m_new

    n_full = qi * (tq // tk)

    def full_tile(ki, carry):
        tile(ki, 0, False)
        return carry

    lax.fori_loop(0, n_full, full_tile, 0)
    for d in range(tq // tk):
        tile(n_full + d, d * tk, True)
    o_ref[...] = (acc_sc[...] / l_sc[...] * g_ref[...].astype(F32)).astype(o_ref.dtype)


def _prompt_attn(q256, k256, v, gate, *, batch, t_pad, tq, tk):
    nq = t_pad // tq
    return pl.pallas_call(
        functools.partial(_prompt_attn_body, tq=tq, tk=tk),
        grid=(batch, N_HEADS, nq),
        in_specs=[
            pl.BlockSpec((tq, QK_PAD), lambda b, h, i: (b * nq + i, h)),
            pl.BlockSpec((t_pad, QK_PAD), lambda b, h, i: (b, h)),
            pl.BlockSpec((t_pad, V_HEAD), lambda b, h, i: (b, h)),
            pl.BlockSpec((tq, V_HEAD), lambda b, h, i: (b * nq + i, h)),
        ],
        out_specs=pl.BlockSpec((tq, V_HEAD), lambda b, h, i: (b * nq + i, h)),
        out_shape=jax.ShapeDtypeStruct((batch * t_pad, ATTN_WIDTH), BF16),
        scratch_shapes=[pltpu.VMEM((tq, LANES), F32), pltpu.VMEM((tq, LANES), F32),
                        pltpu.VMEM((tq, V_HEAD), F32)],
        compiler_params=_cparams(("parallel", "parallel", "arbitrary")),
        name="prompt_attn",
    )(q256, k256, v, gate)


def _sample_attn_body(pt_ref, ql_ref, qr_ref, cn_ref, kn_ref, c_hbm, rt_hbm, o_ref,
                      c_stage, rt_stage, c_sem, rt_sem, *scratch, layer, n_req, n_steps, n_pages, n_sub):
    cbufs, rtbufs = scratch[:n_sub], scratch[n_sub:2 * n_sub]
    m_sc, l_sc, acc_sc = scratch[2 * n_sub:]
    n_slots = c_stage.shape[0]
    depth = n_slots - 1
    step = pl.program_id(1)
    total = n_req * n_steps
    flat = pl.program_id(0) * n_steps + step
    slot = flat % n_slots

    def page_copies(f, lookup):
        sl = f % n_slots
        f = jnp.where(f >= total, f - total, f)
        r, s = f // n_steps, f % n_steps
        copies = []
        for k in range(n_pages):
            pg = pt_ref[r, s * n_pages + k] if lookup else 0
            copies.append(pltpu.make_async_copy(c_hbm.at[layer, pg], c_stage.at[sl, k], c_sem.at[sl]))
            copies.append(pltpu.make_async_copy(rt_hbm.at[layer, pg], rt_stage.at[sl, k], rt_sem.at[sl]))
        return copies

    @pl.when(flat == 0)
    def _():
        for d in range(depth):
            for cp in page_copies(flat + d, True):
                cp.start()

    ql = ql_ref[0]
    qr = qr_ref[0]
    dn_t = (((1,), (1,)), ((), ()))

    @pl.when(step == 0)
    def _():
        cn = cn_ref[0]
        kn = kn_ref[0]
        s = (lax.dot_general(ql, cn, dn_t, preferred_element_type=F32)
             + lax.dot_general(qr, kn, dn_t, preferred_element_type=F32))
        tok = lax.broadcasted_iota(jnp.int32, s.shape, 0) // N_HEADS
        key = lax.broadcasted_iota(jnp.int32, s.shape, 1)
        s = jnp.where(key <= tok, s, -jnp.inf)
        m = jnp.max(s, axis=-1, keepdims=True)
        p = jnp.exp2((s - m) * EXP2_SCALE)
        m_sc[...] = jnp.broadcast_to(m, m_sc.shape)
        l_sc[...] = jnp.broadcast_to(jnp.sum(p, axis=-1, keepdims=True), l_sc.shape)
        acc_sc[...] = jnp.dot(p.astype(BF16), cn, preferred_element_type=F32)

    for cp in page_copies(flat + depth, True):
        cp.start()
    for cp in page_copies(flat, False):
        cp.wait()

    per = n_pages // n_sub
    page = c_stage.shape[2]
    scores = []
    for sub in range(n_sub):
        for i in range(per):
            k = sub * per + i
            cbufs[sub][i * page:(i + 1) * page, :] = c_stage[slot, k].astype(BF16)
            rtbufs[sub][:, i * page:(i + 1) * page] = rt_stage[slot, k].astype(BF16)
        scores.append(lax.dot_general(ql, cbufs[sub][...], dn_t, preferred_element_type=F32)
                      + jnp.dot(qr, rtbufs[sub][...], preferred_element_type=F32))
    m_run, l_run, acc = m_sc[...], l_sc[...], acc_sc[...]
    for sub in range(n_sub):
        s = scores[sub]
        m_new = jnp.maximum(m_run, jnp.max(s, axis=-1, keepdims=True))
        alpha = jnp.exp2((m_run - m_new) * EXP2_SCALE)
        p = jnp.exp2((s - _lane_tile(m_new, per * page)) * EXP2_SCALE)
        l_run = alpha * l_run + jnp.sum(p, axis=-1, keepdims=True)
        acc = (_lane_tile(alpha, KV_RANK) * acc
               + jnp.dot(p.astype(BF16), cbufs[sub][...], preferred_element_type=F32))
        m_run = m_new
    m_sc[...], l_sc[...], acc_sc[...] = m_run, l_run, acc

    @pl.when(step + 1 == n_steps)
    def _():
        o_ref[0] = (acc / _lane_tile(l_run, KV_RANK)).astype(o_ref.dtype)

    @pl.when(flat + 1 == total)
    def _():
        for d in range(1, depth + 1):
            for cp in page_copies(flat + d, False):
                cp.wait()


def _sample_attn(page_table, q_lat, q_rope, cn_pad, kn_pad, cache_c, cache_rt, *, layer):
    nb, n_rows, _ = q_lat.shape
    n_tbl = page_table.shape[1]
    npg = PAGES_PER_STEP
    n_steps = n_tbl // npg
    page = cache_c.shape[2]

    def q_map(b, s, pt):
        return (b, 0, 0)

    in_specs = [
        pl.BlockSpec((1, n_rows, KV_RANK), q_map),
        pl.BlockSpec((1, n_rows, QK_ROPE), q_map),
        pl.BlockSpec((1, NEW_PAD, KV_RANK), q_map),
        pl.BlockSpec((1, NEW_PAD, QK_ROPE), q_map),
        pl.BlockSpec(memory_space=pl.ANY),
        pl.BlockSpec(memory_space=pl.ANY),
    ]
    grid_spec = pltpu.PrefetchScalarGridSpec(
        num_scalar_prefetch=1,
        grid=(nb, n_steps),
        in_specs=in_specs,
        out_specs=pl.BlockSpec((1, n_rows, KV_RANK), q_map),
        scratch_shapes=[
            pltpu.VMEM((SAMPLE_STAGE_SLOTS, npg, page, KV_RANK), F32),
            pltpu.VMEM((SAMPLE_STAGE_SLOTS, npg, QK_ROPE, page), F32),
            pltpu.SemaphoreType.DMA((SAMPLE_STAGE_SLOTS,)),
            pltpu.SemaphoreType.DMA((SAMPLE_STAGE_SLOTS,)),
            *[pltpu.VMEM((npg // SAMPLE_SUB_CHUNKS * page, KV_RANK), BF16)] * SAMPLE_SUB_CHUNKS,
            *[pltpu.VMEM((QK_ROPE, npg // SAMPLE_SUB_CHUNKS * page), BF16)] * SAMPLE_SUB_CHUNKS,
            pltpu.VMEM((n_rows, LANES), F32),
            pltpu.VMEM((n_rows, LANES), F32),
            pltpu.VMEM((n_rows, KV_RANK), F32),
        ],
    )
    return pl.pallas_call(
        functools.partial(_sample_attn_body, layer=layer, n_req=nb, n_steps=n_steps, n_pages=npg,
                          n_sub=SAMPLE_SUB_CHUNKS),
        grid_spec=grid_spec,
        out_shape=jax.ShapeDtypeStruct((nb, n_rows, KV_RANK), F32),
        compiler_params=_cparams(("arbitrary", "arbitrary")),
        name="sample_attn",
    )(page_table, q_lat, q_rope, cn_pad, kn_pad, cache_c, cache_rt)


CONV_HALO = 32
CONV_CHUNK = 128


def _prompt_conv_body(u_ref, w_ref, b_ref, o_ref, ext, *, t_pad):
    tc = u_ref.shape[1]
    ext[0:CONV_HALO, :] = jnp.zeros((CONV_HALO, tc), F32)
    ext[CONV_HALO:CONV_HALO + t_pad, :] = u_ref[...]
    ext[CONV_HALO + t_pad:, :] = jnp.zeros((SUBLANES, tc), F32)
    off = CONV_HALO - (CONV_K - 1)
    span = CONV_CHUNK + SUBLANES
    for c in range(t_pad // CONV_CHUNK):
        base = c * CONV_CHUNK
        acc = jnp.broadcast_to(b_ref[...], (CONV_CHUNK, tc))
        for r in range(SUBLANES):
            z = None
            for k in range(CONV_K):
                if (off + k) % SUBLANES != r:
                    continue
                a = base + off + k - r
                term = ext[a:a + span, :] * w_ref[k:k + 1, :]
                z = term if z is None else z + term
            acc = acc + z[r:r + CONV_CHUNK, :]
        o_ref[base:base + CONV_CHUNK, :] = acc


def _prompt_conv(u, w_dw, b_dw, *, batch, t_pad, tc):
    c = u.shape[1]
    return pl.pallas_call(
        functools.partial(_prompt_conv_body, t_pad=t_pad),
        grid=(batch, c // tc),
        in_specs=[
            pl.BlockSpec((t_pad, tc), lambda b, j: (b, j)),
            pl.BlockSpec((CONV_K, tc), lambda b, j: (0, j)),
            pl.BlockSpec((1, tc), lambda b, j: (0, j)),
        ],
        out_specs=pl.BlockSpec((t_pad, tc), lambda b, j: (b, j)),
        out_shape=jax.ShapeDtypeStruct(u.shape, F32),
        scratch_shapes=[pltpu.VMEM((CONV_HALO + t_pad + SUBLANES, tc), F32)],
        compiler_params=_cparams(("parallel", "parallel")),
        name="prompt_conv",
    )(u, w_dw, b_dw.reshape(1, c))


def _sample_conv_body(st_ref, u_ref, ws_ref, wu_ref, b_ref, o_ref, *, n_new):
    st = st_ref[...]
    u = u_ref[...]
    for t in range(n_new):
        y = jnp.sum(st * ws_ref[t][None], axis=1) + jnp.sum(u * wu_ref[t][None], axis=1)
        o_ref[:, t, :] = y + b_ref[...]


def _sample_conv(state, u, w_dw, b_dw, *, bb, tc):
    nb, hist, c = state.shape
    n_new = u.shape[1]
    w_ext = jnp.stack([jnp.pad(w_dw, ((t, n_new - 1 - t), (0, 0))) for t in range(n_new)])
    w_state = w_ext[:, :hist]
    w_new = w_ext[:, hist:]
    return pl.pallas_call(
        functools.partial(_sample_conv_body, n_new=n_new),
        grid=(nb // bb, c // tc),
        in_specs=[
            pl.BlockSpec((bb, hist, tc), lambda i, j: (i, 0, j)),
            pl.BlockSpec((bb, n_new, tc), lambda i, j: (i, 0, j)),
            pl.BlockSpec((n_new, hist, tc), lambda i, j: (0, 0, j)),
            pl.BlockSpec((n_new, n_new, tc), lambda i, j: (0, 0, j)),
            pl.BlockSpec((1, tc), lambda i, j: (0, j)),
        ],
        out_specs=pl.BlockSpec((bb, n_new, tc), lambda i, j: (i, 0, j)),
        out_shape=jax.ShapeDtypeStruct(u.shape, F32),
        compiler_params=_cparams(("parallel", "parallel")),
        name="sample_conv",
    )(state, u, w_state, w_new, b_dw.reshape(1, c))


def _merge_body(a_ref, conv_ref, sgc_ref, lnw_ref, lnb_ref, woa_ref, woc_ref, ga_ref, gc_ref, o_ref, cg):
    @pl.when(pl.program_id(1) == 0)
    def _():
        x = conv_ref[...]
        mu = jnp.mean(x, axis=-1, keepdims=True)
        xc = x - mu
        y = xc * lax.rsqrt(jnp.mean(xc * xc, axis=-1, keepdims=True) + NORM_EPS)
        y = y * lnw_ref[...] + lnb_ref[...]
        cg[...] = (y * jax.nn.sigmoid(y) * sgc_ref[...].astype(F32)).astype(BF16)

    ya = jnp.dot(a_ref[...], woa_ref[...], preferred_element_type=F32)
    yc = jnp.dot(cg[...], woc_ref[...], preferred_element_type=F32)
    o_ref[...] = (ga_ref[...].astype(F32) * ya + gc_ref[...].astype(F32) * yc).astype(o_ref.dtype)


def _merge(a, conv, sgc, ln_w, ln_b, w_oa, w_oc, g_a, g_c, *, tm, tn):
    rows, d = a.shape
    row_full = pl.BlockSpec((tm, d), lambda i, j: (i, 0))
    vec = pl.BlockSpec((1, d), lambda i, j: (0, 0))
    w_spec = pl.BlockSpec((d, tn), lambda i, j: (0, j))
    tile = pl.BlockSpec((tm, tn), lambda i, j: (i, j))
    return pl.pallas_call(
        _merge_body,
        grid=(rows // tm, d // tn),
        in_specs=[row_full, row_full, row_full, vec, vec, w_spec, w_spec, tile, tile],
        out_specs=tile,
        out_shape=jax.ShapeDtypeStruct((rows, d), BF16),
        scratch_shapes=[pltpu.VMEM((tm, d), BF16)],
        compiler_params=_cparams(("parallel", "arbitrary")),
        name="branch_merge",
    )(a, conv, sgc, ln_w.reshape(1, d), ln_b.reshape(1, d), w_oa, w_oc, g_a, g_c)


def _out_body(m_ref, w_ref, h_ref, g_ref, o_ref):
    y = jnp.dot(m_ref[...], w_ref[...], preferred_element_type=F32)
    yn = y * lax.rsqrt(jnp.mean(y * y, axis=-1, keepdims=True) + NORM_EPS) * g_ref[...]
    o_ref[...] = h_ref[...] + yn


def _out_proj(merged, w_out, h, g_post, *, tm):
    rows, d = h.shape
    row = pl.BlockSpec((tm, d), lambda i: (i, 0))
    return pl.pallas_call(
        _out_body,
        grid=(rows // tm,),
        in_specs=[row, pl.BlockSpec((d, d), lambda i: (0, 0)), row, pl.BlockSpec((1, d), lambda i: (0, 0))],
        out_specs=row,
        out_shape=jax.ShapeDtypeStruct((rows, d), F32),
        compiler_params=_cparams(("parallel",)),
        name="out_proj",
    )(merged, w_out, h, g_post.reshape(1, d))


def _rope_tables128(pos):
    inv = 1.0 / (ROPE_THETA ** (jnp.arange(0, QK_ROPE, 2, dtype=F32) / QK_ROPE))
    ang = pos.astype(F32)[:, None] * inv[None, :]
    cos, sin = jnp.cos(ang), jnp.sin(ang)
    return jnp.concatenate([cos, cos, cos, cos], axis=1), jnp.concatenate([-sin, sin, -sin, sin], axis=1)


def _split_w_in(w_in):
    a = Q_COLS
    b = a + KV_RANK
    c = b + QK_ROPE
    d = c + ATTN_WIDTH
    e = d + 2 * CONV_WIDTH
    f = e + CONV_WIDTH
    w_q, w_ckv, w_kr, w_ga, w_glu, w_gc, w_mg = jnp.split(w_in, [a, b, c, d, e, f], axis=1)
    dm = w_in.shape[0]
    w_q = jnp.pad(w_q.reshape(dm, N_HEADS, QK_HEAD), ((0, 0), (0, 0), (0, QK_PAD - QK_HEAD)))
    w_q = w_q.reshape(dm, N_HEADS * QK_PAD)
    w_kr = jnp.pad(w_kr, ((0, 0), (0, LANES - QK_ROPE)))
    w_glu_a, w_glu_b = jnp.split(w_glu, 2, axis=1)
    w_mga, w_mgc = jnp.split(w_mg, 2, axis=1)
    bf = lambda w: w.astype(BF16)
    return dict(q=bf(w_q), ckv=bf(w_ckv), kr=bf(w_kr), ga=bf(w_ga), glu_a=bf(w_glu_a), glu_b=bf(w_glu_b),
                gc=bf(w_gc), mga=bf(w_mga), mgc=bf(w_mgc))


def _project_all(h, pos, w, g_pre, g_kv, *, tm):
    hn = _rmsnorm(h, g_pre, tm)
    cos_t, sin_t = _rope_tables128(pos)
    hpt = 4
    q256, = _proj(hn, [w["q"]], functools.partial(_ep_q, hpt), [(N_HEADS * QK_PAD, hpt * QK_PAD, BF16)],
                  tm=tm, tns=[hpt * QK_PAD], row_aux=(cos_t, sin_t), name="proj_q")
    ckv, kr, kr_pad = _proj(hn, [w["ckv"], w["kr"]], _ep_ckv_kr,
                            [(KV_RANK, KV_RANK, F32), (QK_ROPE, QK_ROPE, F32), (LANES, LANES, BF16)],
                            tm=tm, tns=[KV_RANK, LANES], vec_aux=(g_kv.reshape(1, KV_RANK),),
                            row_aux=(cos_t, sin_t), name="proj_ckv_kr")
    sga, = _proj(hn, [w["ga"]], _ep_silu, [(ATTN_WIDTH, 1024, BF16)], tm=tm, tns=[1024], name="proj_gate_a")
    u, = _proj(hn, [w["glu_a"], w["glu_b"]], _ep_glu, [(CONV_WIDTH, 1024, F32)], tm=tm, tns=[1024, 1024],
               name="proj_glu")
    sgc, = _proj(hn, [w["gc"]], _ep_silu, [(CONV_WIDTH, 1024, BF16)], tm=tm, tns=[1024], name="proj_gate_c")
    g_a, g_c = _proj(hn, [w["mga"], w["mgc"]], _ep_sigmoid2, [(D_MODEL, 1024, BF16), (D_MODEL, 1024, BF16)],
                     tm=tm, tns=[1024, 1024], name="proj_merge_gates")
    return q256, ckv, kr, kr_pad, sga, u, sgc, g_a, g_c


def _finish(a, conv, sgc, g_a, g_c, h, lw, *, tm_merge, tm_out):
    merged = _merge(a, conv, sgc, lw["ln_w"], lw["ln_b"], lw["w_oa"], lw["w_oc"], g_a, g_c, tm=tm_merge, tn=512)
    return _out_proj(merged, lw["w_out"], h, lw["g_post"], tm=tm_out)


def kernel(x_prompt, x_sample, cache_ckv, cache_krope, state_conv, page_table, meta_tokens, w_norm_pre,
           w_norm_post, w_in, w_kv_norm, w_uk, w_uv, w_o_attn, w_dw, b_dw, w_conv_ln, b_conv_ln, w_o_conv,
           w_out):
    batch, seq, d = x_prompt.shape
    nb, n_new, _ = x_sample.shape
    depth = w_in.shape[0]
    t_real = seq + N_META
    t_pad = -(-t_real // T_TILE) * T_TILE
    past_len = page_table.shape[1] * cache_ckv.shape[2]
    hist = CONV_K - 1

    meta = jnp.broadcast_to(meta_tokens[None].astype(x_prompt.dtype), (batch, N_META, d))
    hp = jnp.concatenate([meta, x_prompt, jnp.zeros((batch, t_pad - t_real, d), x_prompt.dtype)], axis=1)
    hp = hp.reshape(batch * t_pad, d)
    hs = x_sample.reshape(nb * n_new, d)
    pos_p = jnp.arange(t_pad)
    pos_s = jnp.tile(past_len + jnp.arange(n_new), nb)
    tm_p, tm_s = 768, nb * n_new
    cache_krope_t = jnp.swapaxes(cache_krope, 2, 3)

    outs = [[] for _ in range(6)]
    for l in range(depth):
        w = _split_w_in(w_in[l])
        w_uk_l = w_uk[l].reshape(KV_RANK, N_HEADS * QK_NOPE).astype(BF16)
        w_uv_l = w_uv[l].reshape(KV_RANK, N_HEADS * V_HEAD).astype(BF16)
        w_uk_t = jnp.transpose(w_uk[l], (1, 2, 0)).astype(BF16)
        w_uv_h = jnp.transpose(w_uv[l], (1, 0, 2)).astype(BF16)
        lw = dict(ln_w=w_conv_ln[l], ln_b=b_conv_ln[l], w_oa=w_o_attn[l].astype(BF16),
                  w_oc=w_o_conv[l].astype(BF16), w_out=w_out[l].astype(BF16), g_post=w_norm_post[l])

        q256, ckv_p, kr_p, kr_pad, sga, u_p, sgc, g_a, g_c = _project_all(
            hp, pos_p, w, w_norm_pre[l], w_kv_norm[l], tm=tm_p)
        k256, v = _proj(ckv_p, [w_uk_l, w_uv_l], functools.partial(_ep_kv_up, 8),
                        [(N_HEADS * QK_PAD, 8 * QK_PAD, BF16), (ATTN_WIDTH, 8 * V_HEAD, BF16)],
                        tm=tm_p, tns=[8 * QK_NOPE, 8 * V_HEAD], row_aux=(kr_pad,), name="kv_up")
        a_p = _prompt_attn(q256, k256, v, sga, batch=batch, t_pad=t_pad, tq=tm_p, tk=T_TILE)
        conv_p = _prompt_conv(u_p, w_dw[l], b_dw[l], batch=batch, t_pad=t_pad, tc=LANES)
        hp_new = _finish(a_p, conv_p, sgc, g_a, g_c, hp, lw, tm_merge=tm_p, tm_out=512)

        q256, ckv_s, kr_s, _, sga, u_s, sgc, g_a, g_c = _project_all(
            hs, pos_s, w, w_norm_pre[l], w_kv_norm[l], tm=tm_s)
        q4 = q256.reshape(nb * n_new, N_HEADS, QK_PAD)
        q_nope = q4[:, :, :QK_NOPE].reshape(nb * n_new, N_HEADS * QK_NOPE)
        q_rope = q4[:, :, QK_NOPE:QK_HEAD].reshape(nb, n_new * N_HEADS, QK_ROPE)
        q_lat = _head_mm(q_nope, w_uk_t, out_dtype=BF16, name="q_absorb")
        q_lat = q_lat.reshape(nb, n_new * N_HEADS, KV_RANK)
        pad_new = lambda x: jnp.pad(x.reshape(nb, n_new, -1), ((0, 0), (0, NEW_PAD - n_new), (0, 0))).astype(BF16)
        o_lat = _sample_attn(page_table, q_lat, q_rope, pad_new(ckv_s), pad_new(kr_s),
                             cache_ckv, cache_krope_t, layer=l)
        a_s = _head_mm(o_lat.reshape(nb * n_new, N_HEADS * KV_RANK), w_uv_h, sga, out_dtype=BF16, name="v_up")
        conv_s = _sample_conv(state_conv[l], u_s.reshape(nb, n_new, d), w_dw[l], b_dw[l], bb=8, tc=512)
        hs_new = _finish(a_s, conv_s.reshape(nb * n_new, d), sgc, g_a, g_c, hs, lw, tm_merge=tm_s, tm_out=tm_s)

        unpad = lambda x: x.reshape(batch, t_pad, -1)[:, :t_real]
        outs[0].append(unpad(ckv_p))
        outs[1].append(unpad(kr_p))
        outs[2].append(unpad(u_p)[:, t_real - hist:])
        outs[3].append(ckv_s.reshape(nb, n_new, KV_RANK))
        outs[4].append(kr_s.reshape(nb, n_new, QK_ROPE))
        outs[5].append(jnp.concatenate([state_conv[l], u_s.reshape(nb, n_new, d)], axis=1)[:, n_new:])
        hp, hs = hp_new, hs_new

    y_prompt = hp.reshape(batch, t_pad, d)[:, N_META:t_real]
    y_sample = hs.reshape(nb, n_new, d)
    return (y_prompt, y_sample) + tuple(jnp.stack(o) for o in outs)
```

```python
import functools
import math

import jax
import jax.numpy as jnp
from jax import lax
from jax.experimental import pallas as pl
from jax.experimental.pallas import tpu as pltpu

F32 = jnp.float32
BF16 = jnp.bfloat16

D_MODEL = 2048
N_META = 16
N_HEADS = 16
QK_NOPE = 128
QK_ROPE = 64
QK_HEAD = QK_NOPE + QK_ROPE
V_HEAD = 128
KV_RANK = 512
ATTN_WIDTH = N_HEADS * V_HEAD
CONV_WIDTH = D_MODEL
CONV_K = 31
ROPE_THETA = 10000.0
NORM_EPS = 1e-6
ATTN_SCALE = 1.0 / math.sqrt(QK_HEAD)
EXP2_SCALE = ATTN_SCALE * math.log2(math.e)
Q_COLS = N_HEADS * QK_HEAD

LANES = 128
SUBLANES = 8
QK_PAD = 2 * LANES
T_TILE = 256
PAGES_PER_STEP = 16
SAMPLE_STAGE_SLOTS = 3
SAMPLE_SUB_CHUNKS = 4
NEW_PAD = 16
VMEM_LIMIT = 56 * 1024 * 1024


def _cparams(sem):
    return pltpu.CompilerParams(dimension_semantics=sem, vmem_limit_bytes=VMEM_LIMIT)


def _rmsnorm_body(x_ref, g_ref, o_ref):
    x = x_ref[...]
    y = x * lax.rsqrt(jnp.mean(x * x, axis=-1, keepdims=True) + NORM_EPS)
    o_ref[...] = (y * g_ref[...]).astype(o_ref.dtype)


def _rmsnorm(x, g, tm):
    rows, d = x.shape
    return pl.pallas_call(
        _rmsnorm_body,
        grid=(rows // tm,),
        in_specs=[pl.BlockSpec((tm, d), lambda i: (i, 0)), pl.BlockSpec((1, d), lambda i: (0, 0))],
        out_specs=pl.BlockSpec((tm, d), lambda i: (i, 0)),
        out_shape=jax.ShapeDtypeStruct((rows, d), BF16),
        compiler_params=_cparams(("parallel",)),
        name="rmsnorm_pre",
    )(x, g.reshape(1, d))


def _proj_body(epilogue, n_w, n_aux, x_ref, *refs):
    w_refs, aux_refs, out_refs = refs[:n_w], refs[n_w:n_w + n_aux], refs[n_w + n_aux:]
    x = x_ref[...].astype(BF16)
    accs = [jnp.dot(x, w[...], preferred_element_type=F32) for w in w_refs]
    outs = epilogue(accs, [a[...] for a in aux_refs])
    for o_ref, o in zip(out_refs, outs):
        o_ref[...] = o.astype(o_ref.dtype)


def _proj(x, ws, epilogue, outs, *, tm, tns, vec_aux=(), row_aux=(), name):
    rows, k = x.shape
    grid = (rows // tm, ws[0].shape[1] // tns[0])
    in_specs = [pl.BlockSpec((tm, k), lambda i, j: (i, 0))]
    in_specs += [pl.BlockSpec((k, tn), lambda i, j: (0, j)) for tn in tns]
    in_specs += [pl.BlockSpec((1, a.shape[1]), lambda i, j: (0, 0)) for a in vec_aux]
    in_specs += [pl.BlockSpec((tm, a.shape[1]), functools.partial(lambda n, i, j: (i % n, 0), a.shape[0] // tm))
                 for a in row_aux]
    out_specs = [pl.BlockSpec((tm, tw), lambda i, j: (i, j)) for (_, tw, _) in outs]
    out_shape = [jax.ShapeDtypeStruct((rows, w), dt) for (w, _, dt) in outs]
    return pl.pallas_call(
        functools.partial(_proj_body, epilogue, len(ws), len(vec_aux) + len(row_aux)),
        grid=grid, in_specs=in_specs, out_specs=out_specs, out_shape=out_shape,
        compiler_params=_cparams(("parallel", "arbitrary")),
        name=name,
    )(x, *ws, *vec_aux, *row_aux)


def _swap_rope_halves(x):
    lane = lax.broadcasted_iota(jnp.int32, x.shape, 1)
    first_half = (lane % QK_ROPE) < (QK_ROPE // 2)
    return jnp.where(first_half, pltpu.roll(x, LANES - QK_ROPE // 2, 1), pltpu.roll(x, QK_ROPE // 2, 1))


def _rope128(x, cos_t, sin_t):
    return x * cos_t + _swap_rope_halves(x) * sin_t


def _ep_q(heads_per_tile, accs, aux):
    acc, = accs
    cos_t, sin_t = aux
    parts = []
    for h in range(heads_per_tile):
        parts.append(acc[:, h * QK_PAD:h * QK_PAD + LANES])
        parts.append(_rope128(acc[:, h * QK_PAD + LANES:(h + 1) * QK_PAD], cos_t, sin_t))
    return [jnp.concatenate(parts, axis=1)]


def _ep_ckv_kr(accs, aux):
    c, kr = accs
    g, cos_t, sin_t = aux
    ckv = c * lax.rsqrt(jnp.mean(c * c, axis=-1, keepdims=True) + NORM_EPS) * g
    kr = _rope128(kr, cos_t, sin_t)
    return [ckv, kr[:, :QK_ROPE], kr]


def _ep_silu(accs, aux):
    a, = accs
    return [a * jax.nn.sigmoid(a)]


def _ep_glu(accs, aux):
    a, b = accs
    return [a * jax.nn.sigmoid(b)]


def _ep_sigmoid2(accs, aux):
    a, b = accs
    return [jax.nn.sigmoid(a), jax.nn.sigmoid(b)]


def _ep_kv_up(heads_per_tile, accs, aux):
    kn, v = accs
    kr_pad, = aux
    parts = []
    for h in range(heads_per_tile):
        parts.append(kn[:, h * QK_NOPE:(h + 1) * QK_NOPE].astype(BF16))
        parts.append(kr_pad)
    return [jnp.concatenate(parts, axis=1), v]


def _head_mm_body(x_ref, w_ref, o_ref):
    o_ref[...] = jnp.dot(x_ref[...].astype(BF16), w_ref[0], preferred_element_type=F32).astype(o_ref.dtype)


def _head_mm_gated_body(x_ref, w_ref, g_ref, o_ref):
    y = jnp.dot(x_ref[...].astype(BF16), w_ref[0], preferred_element_type=F32)
    o_ref[...] = (y * g_ref[...].astype(F32)).astype(o_ref.dtype)


def _head_mm(x, w, gate=None, *, out_dtype, name):
    rows = x.shape[0]
    nh, k, n = w.shape
    in_specs = [pl.BlockSpec((rows, k), lambda h: (0, h)), pl.BlockSpec((1, k, n), lambda h: (h, 0, 0))]
    args = [x, w]
    body = _head_mm_body
    if gate is not None:
        in_specs.append(pl.BlockSpec((rows, n), lambda h: (0, h)))
        args.append(gate)
        body = _head_mm_gated_body
    return pl.pallas_call(
        body, grid=(nh,), in_specs=in_specs,
        out_specs=pl.BlockSpec((rows, n), lambda h: (0, h)),
        out_shape=jax.ShapeDtypeStruct((rows, nh * n), out_dtype),
        compiler_params=_cparams(("parallel",)),
        name=name,
    )(*args)


def _lane_tile(x, width):
    return jnp.concatenate([x] * (width // LANES), axis=1)


def _prompt_attn_body(q_ref, k_ref, v_ref, g_ref, o_ref, m_sc, l_sc, acc_sc, *, tq, tk):
    qi = pl.program_id(2)
    m_sc[...] = jnp.full(m_sc.shape, -jnp.inf, F32)
    l_sc[...] = jnp.zeros(l_sc.shape, F32)
    acc_sc[...] = jnp.zeros(acc_sc.shape, F32)

    def scores(ki, row0):
        ks = pl.multiple_of(ki * tk, tk)
        return lax.dot_general(q_ref[pl.ds(row0, tq - row0), :], k_ref[pl.ds(ks, tk), :],
                               (((1,), (1,)), ((), ())), preferred_element_type=F32)

    def update(s, ki, row0, masked):
        rows = pl.ds(row0, tq - row0)
        ks = pl.multiple_of(ki * tk, tk)
        if masked:
            q_pos = qi * tq + row0 + lax.broadcasted_iota(jnp.int32, s.shape, 0)
            k_pos = ks + lax.broadcasted_iota(jnp.int32, s.shape, 1)
            s = jnp.where(k_pos <= q_pos, s, -jnp.inf)
        m_old = m_sc[rows, :]
        m_new = jnp.maximum(m_old, jnp.max(s, axis=-1, keepdims=True))
        alpha = jnp.exp2((m_old - m_new) * EXP2_SCALE)
        p = jnp.exp2((s - _lane_tile(m_new, tk)) * EXP2_SCALE)
        l_sc[rows, :] = alpha * l_sc[rows, :] + jnp.sum(p, axis=-1, keepdims=True)
        acc_sc[rows, :] = alpha * acc_sc[rows, :] + jnp.dot(p.astype(BF16), v_ref[pl.ds(ks, tk), :],
                                                            preferred_element_type=F32)
        m_sc[rows, :] = m_new

    group = tq // tk

    def full_group(gi, carry):
        ss = [scores(gi * group + j, 0) for j in range(group)]
        for j in range(group):
            update(ss[j], gi * group + j, 0, False)
        return carry

    lax.fori_loop(0, qi, full_group, 0)
    ss = [scores(qi * group + d, d * tk) for d in range(group)]
    for d in range(group):
        update(ss[d], qi * group + d, d * tk, True)
    o_ref[...] = (acc_sc[...] / l_sc[...] * g_ref[...].astype(F32)).astype(o_ref.dtype)


def _prompt_attn(q256, k256, v, gate, *, batch, t_pad, tq, tk):
    nq = t_pad // tq
    return pl.pallas_call(
        functools.partial(_prompt_attn_body, tq=tq, tk=tk),
        grid=(batch, N_HEADS, nq),
        in_specs=[
            pl.BlockSpec((tq, QK_PAD), lambda b, h, i: (b * nq + i, h)),
            pl.BlockSpec((t_pad, QK_PAD), lambda b, h, i: (b, h)),
            pl.BlockSpec((t_pad, V_HEAD), lambda b, h, i: (b, h)),
            pl.BlockSpec((tq, V_HEAD), lambda b, h, i: (b * nq + i, h)),
        ],
        out_specs=pl.BlockSpec((tq, V_HEAD), lambda b, h, i: (b * nq + i, h)),
        out_shape=jax.ShapeDtypeStruct((batch * t_pad, ATTN_WIDTH), BF16),
        scratch_shapes=[pltpu.VMEM((tq, LANES), F32), pltpu.VMEM((tq, LANES), F32),
                        pltpu.VMEM((tq, V_HEAD), F32)],
        compiler_params=_cparams(("parallel", "parallel", "arbitrary")),
        name="prompt_attn",
    )(q256, k256, v, gate)


def _sample_attn_body(pt_ref, ql_ref, qr_ref, cn_ref, kn_ref, c_hbm, rt_hbm, o_ref,
                      c_stage, rt_stage, c_sem, rt_sem, *scratch, layer, n_req, n_steps, n_pages, n_sub):
    cbufs, rtbufs = scratch[:n_sub], scratch[n_sub:2 * n_sub]
    m_sc, l_sc, acc_sc = scratch[2 * n_sub:]
    n_slots = c_stage.shape[0]
    depth = n_slots - 1
    step = pl.program_id(1)
    total = n_req * n_steps
    flat = pl.program_id(0) * n_steps + step
    slot = flat % n_slots

    def page_copies(f, lookup):
        sl = f % n_slots
        f = jnp.where(f >= total, f - total, f)
        r, s = f // n_steps, f % n_steps
        copies = []
        for k in range(n_pages):
            pg = pt_ref[r, s * n_pages + k] if lookup else 0
            copies.append(pltpu.make_async_copy(c_hbm.at[layer, pg], c_stage.at[sl, k], c_sem.at[sl]))
            copies.append(pltpu.make_async_copy(rt_hbm.at[layer, pg], rt_stage.at[sl, k], rt_sem.at[sl]))
        return copies

    @pl.when(flat == 0)
    def _():
        for d in range(depth):
            for cp in page_copies(flat + d, True):
                cp.start()

    ql = ql_ref[0]
    qr = qr_ref[0]
    dn_t = (((1,), (1,)), ((), ()))

    @pl.when(step == 0)
    def _():
        cn = cn_ref[0]
        kn = kn_ref[0]
        s = (lax.dot_general(ql, cn, dn_t, preferred_element_type=F32)
             + lax.dot_general(qr, kn, dn_t, preferred_element_type=F32))
        tok = lax.broadcasted_iota(jnp.int32, s.shape, 0) // N_HEADS
        key = lax.broadcasted_iota(jnp.int32, s.shape, 1)
        s = jnp.where(key <= tok, s, -jnp.inf)
        m = jnp.max(s, axis=-1, keepdims=True)
        p = jnp.exp2((s - m) * EXP2_SCALE)
        m_sc[...] = jnp.broadcast_to(m, m_sc.shape)
        l_sc[...] = jnp.broadcast_to(jnp.sum(p, axis=-1, keepdims=True), l_sc.shape)
        acc_sc[...] = jnp.dot(p.astype(BF16), cn, preferred_element_type=F32)

    for cp in page_copies(flat + depth, True):
        cp.start()
    for cp in page_copies(flat, False):
        cp.wait()

    per = n_pages // n_sub
    page = c_stage.shape[2]
    scores = []
    for sub in range(n_sub):
        for i in range(per):
            k = sub * per + i
            cbufs[sub][i * page:(i + 1) * page, :] = c_stage[slot, k].astype(BF16)
            rtbufs[sub][:, i * page:(i + 1) * page] = rt_stage[slot, k].astype(BF16)
        scores.append(lax.dot_general(ql, cbufs[sub][...], dn_t, preferred_element_type=F32)
                      + jnp.dot(qr, rtbufs[sub][...], preferred_element_type=F32))
    m_run, l_run, acc = m_sc[...], l_sc[...], acc_sc[...]
    for sub in range(n_sub):
        s = scores[sub]
        m_new = jnp.maximum(m_run, jnp.max(s, axis=-1, keepdims=True))
        alpha = jnp.exp2((m_run - m_new) * EXP2_SCALE)
        p = jnp.exp2((s - _lane_tile(m_new, per * page)) * EXP2_SCALE)
        l_run = alpha * l_run + jnp.sum(p, axis=-1, keepdims=True)
        acc = (_lane_tile(alpha, KV_RANK) * acc
               + jnp.dot(p.astype(BF16), cbufs[sub][...], preferred_element_type=F32))
        m_run = m_new
    m_sc[...], l_sc[...], acc_sc[...] = m_run, l_run, acc

    @pl.when(step + 1 == n_steps)
    def _():
        o_ref[0] = (acc / _lane_tile(l_run, KV_RANK)).astype(o_ref.dtype)

    @pl.when(flat + 1 == total)
    def _():
        for d in range(1, depth + 1):
            for cp in page_copies(flat + d, False):
                cp.wait()


def _sample_attn(page_table, q_lat, q_rope, cn_pad, kn_pad, cache_c, cache_rt, *, layer):
    nb, n_rows, _ = q_lat.shape
    n_tbl = page_table.shape[1]
    npg = PAGES_PER_STEP
    n_steps = n_tbl // npg
    page = cache_c.shape[2]

    def q_map(b, s, pt):
        return (b, 0, 0)

    in_specs = [
        pl.BlockSpec((1, n_rows, KV_RANK), q_map),
        pl.BlockSpec((1, n_rows, QK_ROPE), q_map),
        pl.BlockSpec((1, NEW_PAD, KV_RANK), q_map),
        pl.BlockSpec((1, NEW_PAD, QK_ROPE), q_map),
        pl.BlockSpec(memory_space=pl.ANY),
        pl.BlockSpec(memory_space=pl.ANY),
    ]
    grid_spec = pltpu.PrefetchScalarGridSpec(
        num_scalar_prefetch=1,
        grid=(nb, n_steps),
        in_specs=in_specs,
        out_specs=pl.BlockSpec((1, n_rows, KV_RANK), q_map),
        scratch_shapes=[
            pltpu.VMEM((SAMPLE_STAGE_SLOTS, npg, page, KV_RANK), F32),
            pltpu.VMEM((SAMPLE_STAGE_SLOTS, npg, QK_ROPE, page), F32),
            pltpu.SemaphoreType.DMA((SAMPLE_STAGE_SLOTS,)),
            pltpu.SemaphoreType.DMA((SAMPLE_STAGE_SLOTS,)),
            *[pltpu.VMEM((npg // SAMPLE_SUB_CHUNKS * page, KV_RANK), BF16)] * SAMPLE_SUB_CHUNKS,
            *[pltpu.VMEM((QK_ROPE, npg // SAMPLE_SUB_CHUNKS * page), BF16)] * SAMPLE_SUB_CHUNKS,
            pltpu.VMEM((n_rows, LANES), F32),
            pltpu.VMEM((n_rows, LANES), F32),
            pltpu.VMEM((n_rows, KV_RANK), F32),
        ],
    )
    return pl.pallas_call(
        functools.partial(_sample_attn_body, layer=layer, n_req=nb, n_steps=n_steps, n_pages=npg,
                          n_sub=SAMPLE_SUB_CHUNKS),
        grid_spec=grid_spec,
        out_shape=jax.ShapeDtypeStruct((nb, n_rows, KV_RANK), F32),
        compiler_params=_cparams(("arbitrary", "arbitrary")),
        name="sample_attn",
    )(page_table, q_lat, q_rope, cn_pad, kn_pad, cache_c, cache_rt)


CONV_HALO = 32
CONV_CHUNK = 128


def _prompt_conv_body(u_ref, w_ref, b_ref, o_ref, ext, *, t_pad):
    tc = u_ref.shape[1]
    ext[0:CONV_HALO, :] = jnp.zeros((CONV_HALO, tc), F32)
    ext[CONV_HALO:CONV_HALO + t_pad, :] = u_ref[...]
    ext[CONV_HALO + t_pad:, :] = jnp.zeros((SUBLANES, tc), F32)
    off = CONV_HALO - (CONV_K - 1)
    span = CONV_CHUNK + SUBLANES
    for c in range(t_pad // CONV_CHUNK):
        base = c * CONV_CHUNK
        acc = jnp.broadcast_to(b_ref[...], (CONV_CHUNK, tc))
        for r in range(SUBLANES):
            z = None
            for k in range(CONV_K):
                if (off + k) % SUBLANES != r:
                    continue
                a = base + off + k - r
                term = ext[a:a + span, :] * w_ref[k:k + 1, :]
                z = term if z is None else z + term
            acc = acc + z[r:r + CONV_CHUNK, :]
        o_ref[base:base + CONV_CHUNK, :] = acc


def _prompt_conv(u, w_dw, b_dw, *, batch, t_pad, tc):
    c = u.shape[1]
    return pl.pallas_call(
        functools.partial(_prompt_conv_body, t_pad=t_pad),
        grid=(batch, c // tc),
        in_specs=[
            pl.BlockSpec((t_pad, tc), lambda b, j: (b, j)),
            pl.BlockSpec((CONV_K, tc), lambda b, j: (0, j)),
            pl.BlockSpec((1, tc), lambda b, j: (0, j)),
        ],
        out_specs=pl.BlockSpec((t_pad, tc), lambda b, j: (b, j)),
        out_shape=jax.ShapeDtypeStruct(u.shape, F32),
        scratch_shapes=[pltpu.VMEM((CONV_HALO + t_pad + SUBLANES, tc), F32)],
        compiler_params=_cparams(("parallel", "parallel")),
        name="prompt_conv",
    )(u, w_dw, b_dw.reshape(1, c))


def _sample_conv_body(st_ref, u_ref, ws_ref, wu_ref, b_ref, o_ref, *, n_new):
    st = st_ref[...]
    u = u_ref[...]
    for t in range(n_new):
        y = jnp.sum(st * ws_ref[t][None], axis=1) + jnp.sum(u * wu_ref[t][None], axis=1)
        o_ref[:, t, :] = y + b_ref[...]


def _sample_conv(state, u, w_dw, b_dw, *, bb, tc):
    nb, hist, c = state.shape
    n_new = u.shape[1]
    w_ext = jnp.stack([jnp.pad(w_dw, ((t, n_new - 1 - t), (0, 0))) for t in range(n_new)])
    w_state = w_ext[:, :hist]
    w_new = w_ext[:, hist:]
    return pl.pallas_call(
        functools.partial(_sample_conv_body, n_new=n_new),
        grid=(nb // bb, c // tc),
        in_specs=[
            pl.BlockSpec((bb, hist, tc), lambda i, j: (i, 0, j)),
            pl.BlockSpec((bb, n_new, tc), lambda i, j: (i, 0, j)),
            pl.BlockSpec((n_new, hist, tc), lambda i, j: (0, 0, j)),
            pl.BlockSpec((n_new, n_new, tc), lambda i, j: (0, 0, j)),
            pl.BlockSpec((1, tc), lambda i, j: (0, j)),
        ],
        out_specs=pl.BlockSpec((bb, n_new, tc), lambda i, j: (i, 0, j)),
        out_shape=jax.ShapeDtypeStruct(u.shape, F32),
        compiler_params=_cparams(("parallel", "parallel")),
        name="sample_conv",
    )(state, u, w_state, w_new, b_dw.reshape(1, c))


def _merge_body(a_ref, conv_ref, sgc_ref, lnw_ref, lnb_ref, woa_ref, woc_ref, ga_ref, gc_ref, o_ref, cg):
    @pl.when(pl.program_id(1) == 0)
    def _():
        x = conv_ref[...]
        mu = jnp.mean(x, axis=-1, keepdims=True)
        xc = x - mu
        y = xc * lax.rsqrt(jnp.mean(xc * xc, axis=-1, keepdims=True) + NORM_EPS)
        y = y * lnw_ref[...] + lnb_ref[...]
        cg[...] = (y * jax.nn.sigmoid(y) * sgc_ref[...].astype(F32)).astype(BF16)

    ya = jnp.dot(a_ref[...], woa_ref[...], preferred_element_type=F32)
    yc = jnp.dot(cg[...], woc_ref[...], preferred_element_type=F32)
    o_ref[...] = (ga_ref[...].astype(F32) * ya + gc_ref[...].astype(F32) * yc).astype(o_ref.dtype)


def _merge(a, conv, sgc, ln_w, ln_b, w_oa, w_oc, g_a, g_c, *, tm, tn):
    rows, d = a.shape
    row_full = pl.BlockSpec((tm, d), lambda i, j: (i, 0))
    vec = pl.BlockSpec((1, d), lambda i, j: (0, 0))
    w_spec = pl.BlockSpec((d, tn), lambda i, j: (0, j))
    tile = pl.BlockSpec((tm, tn), lambda i, j: (i, j))
    return pl.pallas_call(
        _merge_body,
        grid=(rows // tm, d // tn),
        in_specs=[row_full, row_full, row_full, vec, vec, w_spec, w_spec, tile, tile],
        out_specs=tile,
        out_shape=jax.ShapeDtypeStruct((rows, d), BF16),
        scratch_shapes=[pltpu.VMEM((tm, d), BF16)],
        compiler_params=_cparams(("parallel", "arbitrary")),
        name="branch_merge",
    )(a, conv, sgc, ln_w.reshape(1, d), ln_b.reshape(1, d), w_oa, w_oc, g_a, g_c)


def _out_body(m_ref, w_ref, h_ref, g_ref, o_ref):
    y = jnp.dot(m_ref[...], w_ref[...], preferred_element_type=F32)
    yn = y * lax.rsqrt(jnp.mean(y * y, axis=-1, keepdims=True) + NORM_EPS) * g_ref[...]
    o_ref[...] = h_ref[...] + yn


def _out_proj(merged, w_out, h, g_post, *, tm):
    rows, d = h.shape
    row = pl.BlockSpec((tm, d), lambda i: (i, 0))
    return pl.pallas_call(
        _out_body,
        grid=(rows // tm,),
        in_specs=[row, pl.BlockSpec((d, d), lambda i: (0, 0)), row, pl.BlockSpec((1, d), lambda i: (0, 0))],
        out_specs=row,
        out_shape=jax.ShapeDtypeStruct((rows, d), F32),
        compiler_params=_cparams(("parallel",)),
        name="out_proj",
    )(merged, w_out, h, g_post.reshape(1, d))


def _rope_tables128(pos):
    inv = 1.0 / (ROPE_THETA ** (jnp.arange(0, QK_ROPE, 2, dtype=F32) / QK_ROPE))
    ang = pos.astype(F32)[:, None] * inv[None, :]
    cos, sin = jnp.cos(ang), jnp.sin(ang)
    return jnp.concatenate([cos, cos, cos, cos], axis=1), jnp.concatenate([-sin, sin, -sin, sin], axis=1)


def _split_w_in(w_in):
    a = Q_COLS
    b = a + KV_RANK
    c = b + QK_ROPE
    d = c + ATTN_WIDTH
    e = d + 2 * CONV_WIDTH
    f = e + CONV_WIDTH
    w_q, w_ckv, w_kr, w_ga, w_glu, w_gc, w_mg = jnp.split(w_in, [a, b, c, d, e, f], axis=1)
    dm = w_in.shape[0]
    w_q = jnp.pad(w_q.reshape(dm, N_HEADS, QK_HEAD), ((0, 0), (0, 0), (0, QK_PAD - QK_HEAD)))
    w_q = w_q.reshape(dm, N_HEADS * QK_PAD)
    w_kr = jnp.pad(w_kr, ((0, 0), (0, LANES - QK_ROPE)))
    w_glu_a, w_glu_b = jnp.split(w_glu, 2, axis=1)
    w_mga, w_mgc = jnp.split(w_mg, 2, axis=1)
    bf = lambda w: w.astype(BF16)
    return dict(q=bf(w_q), ckv=bf(w_ckv), kr=bf(w_kr), ga=bf(w_ga), glu_a=bf(w_glu_a), glu_b=bf(w_glu_b),
                gc=bf(w_gc), mga=bf(w_mga), mgc=bf(w_mgc))


def _project_all(h, pos, w, g_pre, g_kv, *, tm):
    hn = _rmsnorm(h, g_pre, tm)
    cos_t, sin_t = _rope_tables128(pos)
    hpt = 4
    q256, = _proj(hn, [w["q"]], functools.partial(_ep_q, hpt), [(N_HEADS * QK_PAD, hpt * QK_PAD, BF16)],
                  tm=tm, tns=[hpt * QK_PAD], row_aux=(cos_t, sin_t), name="proj_q")
    ckv, kr, kr_pad = _proj(hn, [w["ckv"], w["kr"]], _ep_ckv_kr,
                            [(KV_RANK, KV_RANK, F32), (QK_ROPE, QK_ROPE, F32), (LANES, LANES, BF16)],
                            tm=tm, tns=[KV_RANK, LANES], vec_aux=(g_kv.reshape(1, KV_RANK),),
                            row_aux=(cos_t, sin_t), name="proj_ckv_kr")
    sga, = _proj(hn, [w["ga"]], _ep_silu, [(ATTN_WIDTH, 1024, BF16)], tm=tm, tns=[1024], name="proj_gate_a")
    u, = _proj(hn, [w["glu_a"], w["glu_b"]], _ep_glu, [(CONV_WIDTH, 1024, F32)], tm=tm, tns=[1024, 1024],
               name="proj_glu")
    sgc, = _proj(hn, [w["gc"]], _ep_silu, [(CONV_WIDTH, 1024, BF16)], tm=tm, tns=[1024], name="proj_gate_c")
    g_a, g_c = _proj(hn, [w["mga"], w["mgc"]], _ep_sigmoid2, [(D_MODEL, 1024, BF16), (D_MODEL, 1024, BF16)],
                     tm=tm, tns=[1024, 1024], name="proj_merge_gates")
    return q256, ckv, kr, kr_pad, sga, u, sgc, g_a, g_c


def _finish(a, conv, sgc, g_a, g_c, h, lw, *, tm_merge, tm_out):
    merged = _merge(a, conv, sgc, lw["ln_w"], lw["ln_b"], lw["w_oa"], lw["w_oc"], g_a, g_c, tm=tm_merge, tn=512)
    return _out_proj(merged, lw["w_out"], h, lw["g_post"], tm=tm_out)


def kernel(x_prompt, x_sample, cache_ckv, cache_krope, state_conv, page_table, meta_tokens, w_norm_pre,
           w_norm_post, w_in, w_kv_norm, w_uk, w_uv, w_o_attn, w_dw, b_dw, w_conv_ln, b_conv_ln, w_o_conv,
           w_out):
    batch, seq, d = x_prompt.shape
    nb, n_new, _ = x_sample.shape
    depth = w_in.shape[0]
    t_real = seq + N_META
    t_pad = -(-t_real // T_TILE) * T_TILE
    past_len = page_table.shape[1] * cache_ckv.shape[2]
    hist = CONV_K - 1

    meta = jnp.broadcast_to(meta_tokens[None].astype(x_prompt.dtype), (batch, N_META, d))
    hp = jnp.concatenate([meta, x_prompt, jnp.zeros((batch, t_pad - t_real, d), x_prompt.dtype)], axis=1)
    hp = hp.reshape(batch * t_pad, d)
    hs = x_sample.reshape(nb * n_new, d)
    pos_p = jnp.arange(t_pad)
    pos_s = jnp.tile(past_len + jnp.arange(n_new), nb)
    tm_p, tm_s = 768, nb * n_new
    cache_krope_t = jnp.swapaxes(cache_krope, 2, 3)

    outs = [[] for _ in range(6)]
    for l in range(depth):
        w = _split_w_in(w_in[l])
        w_uk_l = w_uk[l].reshape(KV_RANK, N_HEADS * QK_NOPE).astype(BF16)
        w_uv_l = w_uv[l].reshape(KV_RANK, N_HEADS * V_HEAD).astype(BF16)
        w_uk_t = jnp.transpose(w_uk[l], (1, 2, 0)).astype(BF16)
        w_uv_h = jnp.transpose(w_uv[l], (1, 0, 2)).astype(BF16)
        lw = dict(ln_w=w_conv_ln[l], ln_b=b_conv_ln[l], w_oa=w_o_attn[l].astype(BF16),
                  w_oc=w_o_conv[l].astype(BF16), w_out=w_out[l].astype(BF16), g_post=w_norm_post[l])

        q256, ckv_p, kr_p, kr_pad, sga, u_p, sgc, g_a, g_c = _project_all(
            hp, pos_p, w, w_norm_pre[l], w_kv_norm[l], tm=tm_p)
        k256, v = _proj(ckv_p, [w_uk_l, w_uv_l], functools.partial(_ep_kv_up, 8),
                        [(N_HEADS * QK_PAD, 8 * QK_PAD, BF16), (ATTN_WIDTH, 8 * V_HEAD, BF16)],
                        tm=tm_p, tns=[8 * QK_NOPE, 8 * V_HEAD], row_aux=(kr_pad,), name="kv_up")
        a_p = _prompt_attn(q256, k256, v, sga, batch=batch, t_pad=t_pad, tq=tm_p, tk=T_TILE)
        conv_p = _prompt_conv(u_p, w_dw[l], b_dw[l], batch=batch, t_pad=t_pad, tc=LANES)
        hp_new = _finish(a_p, conv_p, sgc, g_a, g_c, hp, lw, tm_merge=tm_p, tm_out=512)

        q256, ckv_s, kr_s, _, sga, u_s, sgc, g_a, g_c = _project_all(
            hs, pos_s, w, w_norm_pre[l], w_kv_norm[l], tm=tm_s)
        q4 = q256.reshape(nb * n_new, N_HEADS, QK_PAD)
        q_nope = q4[:, :, :QK_NOPE].reshape(nb * n_new, N_HEADS * QK_NOPE)
        q_rope = q4[:, :, QK_NOPE:QK_HEAD].reshape(nb, n_new * N_HEADS, QK_ROPE)
        q_lat = _head_mm(q_nope, w_uk_t, out_dtype=BF16, name="q_absorb")
        q_lat = q_lat.reshape(nb, n_new * N_HEADS, KV_RANK)
        pad_new = lambda x: jnp.pad(x.reshape(nb, n_new, -1), ((0, 0), (0, NEW_PAD - n_new), (0, 0))).astype(BF16)
        o_lat = _sample_attn(page_table, q_lat, q_rope, pad_new(ckv_s), pad_new(kr_s),
                             cache_ckv, cache_krope_t, layer=l)
        a_s = _head_mm(o_lat.reshape(nb * n_new, N_HEADS * KV_RANK), w_uv_h, sga, out_dtype=BF16, name="v_up")
        conv_s = _sample_conv(state_conv[l], u_s.reshape(nb, n_new, d), w_dw[l], b_dw[l], bb=8, tc=512)
        hs_new = _finish(a_s, conv_s.reshape(nb * n_new, d), sgc, g_a, g_c, hs, lw, tm_merge=tm_s, tm_out=tm_s)

        unpad = lambda x: x.reshape(batch, t_pad, -1)[:, :t_real]
        outs[0].append(unpad(ckv_p))
        outs[1].append(unpad(kr_p))
        outs[2].append(unpad(u_p)[:, t_real - hist:])
        outs[3].append(ckv_s.reshape(nb, n_new, KV_RANK))
        outs[4].append(kr_s.reshape(nb, n_new, QK_ROPE))
        outs[5].append(jnp.concatenate([state_conv[l], u_s.reshape(nb, n_new, d)], axis=1)[:, n_new:])
        hp, hs = hp_new, hs_new

    y_prompt = hp.reshape(batch, t_pad, d)[:, N_META:t_real]
    y_sample = hs.reshape(nb, n_new, d)
    return (y_prompt, y_sample) + tuple(jnp.stack(o) for o in outs)
```

```python
import functools
import math

import jax
import jax.numpy as jnp
from jax import lax
from jax.experimental import pallas as pl
from jax.experimental.pallas import tpu as pltpu

F32 = jnp.float32
BF16 = jnp.bfloat16

D_MODEL = 2048
N_META = 16
N_HEADS = 16
QK_NOPE = 128
QK_ROPE = 64
QK_HEAD = QK_NOPE + QK_ROPE
V_HEAD = 128
KV_RANK = 512
ATTN_WIDTH = N_HEADS * V_HEAD
CONV_WIDTH = D_MODEL
CONV_K = 31
ROPE_THETA = 10000.0
NORM_EPS = 1e-6
ATTN_SCALE = 1.0 / math.sqrt(QK_HEAD)
EXP2_SCALE = ATTN_SCALE * math.log2(math.e)
Q_COLS = N_HEADS * QK_HEAD

LANES = 128
SUBLANES = 8
QK_PAD = 2 * LANES
T_TILE = 256
PAGES_PER_STEP = 32
SAMPLE_STAGE_SLOTS = 3
SAMPLE_SUB_CHUNKS = 8
NEW_PAD = 16
VMEM_LIMIT = 56 * 1024 * 1024


def _cparams(sem):
    return pltpu.CompilerParams(dimension_semantics=sem, vmem_limit_bytes=VMEM_LIMIT)


def _rmsnorm_body(x_ref, g_ref, o_ref):
    x = x_ref[...]
    y = x * lax.rsqrt(jnp.mean(x * x, axis=-1, keepdims=True) + NORM_EPS)
    o_ref[...] = (y * g_ref[...]).astype(o_ref.dtype)


def _rmsnorm(x, g, tm):
    rows, d = x.shape
    return pl.pallas_call(
        _rmsnorm_body,
        grid=(rows // tm,),
        in_specs=[pl.BlockSpec((tm, d), lambda i: (i, 0)), pl.BlockSpec((1, d), lambda i: (0, 0))],
        out_specs=pl.BlockSpec((tm, d), lambda i: (i, 0)),
        out_shape=jax.ShapeDtypeStruct((rows, d), BF16),
        compiler_params=_cparams(("parallel",)),
        name="rmsnorm_pre",
    )(x, g.reshape(1, d))


def _proj_body(epilogue, n_w, n_aux, x_ref, *refs):
    w_refs, aux_refs, out_refs = refs[:n_w], refs[n_w:n_w + n_aux], refs[n_w + n_aux:]
    x = x_ref[...].astype(BF16)
    accs = [jnp.dot(x, w[...], preferred_element_type=F32) for w in w_refs]
    outs = epilogue(accs, [a[...] for a in aux_refs])
    for o_ref, o in zip(out_refs, outs):
        o_ref[...] = o.astype(o_ref.dtype)


def _proj(x, ws, epilogue, outs, *, tm, tns, vec_aux=(), row_aux=(), name):
    rows, k = x.shape
    grid = (rows // tm, ws[0].shape[1] // tns[0])
    in_specs = [pl.BlockSpec((tm, k), lambda i, j: (i, 0))]
    in_specs += [pl.BlockSpec((k, tn), lambda i, j: (0, j)) for tn in tns]
    in_specs += [pl.BlockSpec((1, a.shape[1]), lambda i, j: (0, 0)) for a in vec_aux]
    in_specs += [pl.BlockSpec((tm, a.shape[1]), functools.partial(lambda n, i, j: (i % n, 0), a.shape[0] // tm))
                 for a in row_aux]
    out_specs = [pl.BlockSpec((tm, tw), lambda i, j: (i, j)) for (_, tw, _) in outs]
    out_shape = [jax.ShapeDtypeStruct((rows, w), dt) for (w, _, dt) in outs]
    return pl.pallas_call(
        functools.partial(_proj_body, epilogue, len(ws), len(vec_aux) + len(row_aux)),
        grid=grid, in_specs=in_specs, out_specs=out_specs, out_shape=out_shape,
        compiler_params=_cparams(("parallel", "arbitrary")),
        name=name,
    )(x, *ws, *vec_aux, *row_aux)


def _swap_rope_halves(x):
    lane = lax.broadcasted_iota(jnp.int32, x.shape, 1)
    first_half = (lane % QK_ROPE) < (QK_ROPE // 2)
    return jnp.where(first_half, pltpu.roll(x, LANES - QK_ROPE // 2, 1), pltpu.roll(x, QK_ROPE // 2, 1))


def _rope128(x, cos_t, sin_t):
    return x * cos_t + _swap_rope_halves(x) * sin_t


def _ep_q(heads_per_tile, accs, aux):
    acc, = accs
    cos_t, sin_t = aux
    parts = []
    for h in range(heads_per_tile):
        parts.append(acc[:, h * QK_PAD:h * QK_PAD + LANES])
        parts.append(_rope128(acc[:, h * QK_PAD + LANES:(h + 1) * QK_PAD], cos_t, sin_t))
    return [jnp.concatenate(parts, axis=1)]


def _ep_ckv_kr(accs, aux):
    c, kr = accs
    g, cos_t, sin_t = aux
    ckv = c * lax.rsqrt(jnp.mean(c * c, axis=-1, keepdims=True) + NORM_EPS) * g
    kr = _rope128(kr, cos_t, sin_t)
    return [ckv, kr[:, :QK_ROPE], kr]


def _ep_silu(accs, aux):
    a, = accs
    return [a * jax.nn.sigmoid(a)]


def _ep_glu(accs, aux):
    a, b = accs
    return [a * jax.nn.sigmoid(b)]


def _ep_sigmoid2(accs, aux):
    a, b = accs
    return [jax.nn.sigmoid(a), jax.nn.sigmoid(b)]


def _ep_kv_up(heads_per_tile, accs, aux):
    kn, v = accs
    kr_pad, = aux
    parts = []
    for h in range(heads_per_tile):
        parts.append(kn[:, h * QK_NOPE:(h + 1) * QK_NOPE].astype(BF16))
        parts.append(kr_pad)
    return [jnp.concatenate(parts, axis=1), v]


def _head_mm_body(x_ref, w_ref, o_ref):
    o_ref[...] = jnp.dot(x_ref[...].astype(BF16), w_ref[0], preferred_element_type=F32).astype(o_ref.dtype)


def _head_mm_gated_body(x_ref, w_ref, g_ref, o_ref):
    y = jnp.dot(x_ref[...].astype(BF16), w_ref[0], preferred_element_type=F32)
    o_ref[...] = (y * g_ref[...].astype(F32)).astype(o_ref.dtype)


def _head_mm(x, w, gate=None, *, out_dtype, name):
    rows = x.shape[0]
    nh, k, n = w.shape
    in_specs = [pl.BlockSpec((rows, k), lambda h: (0, h)), pl.BlockSpec((1, k, n), lambda h: (h, 0, 0))]
    args = [x, w]
    body = _head_mm_body
    if gate is not None:
        in_specs.append(pl.BlockSpec((rows, n), lambda h: (0, h)))
        args.append(gate)
        body = _head_mm_gated_body
    return pl.pallas_call(
        body, grid=(nh,), in_specs=in_specs,
        out_specs=pl.BlockSpec((rows, n), lambda h: (0, h)),
        out_shape=jax.ShapeDtypeStruct((rows, nh * n), out_dtype),
        compiler_params=_cparams(("parallel",)),
        name=name,
    )(*args)


def _lane_tile(x, width):
    return jnp.concatenate([x] * (width // LANES), axis=1)


def _prompt_attn_body(q_ref, k_ref, v_ref, g_ref, o_ref, m_sc, l_sc, acc_sc, *, tq, tk):
    qi = pl.program_id(2)
    m_sc[...] = jnp.full(m_sc.shape, -jnp.inf, F32)
    l_sc[...] = jnp.zeros(l_sc.shape, F32)
    acc_sc[...] = jnp.zeros(acc_sc.shape, F32)

    def scores(ki, row0):
        ks = pl.multiple_of(ki * tk, tk)
        return lax.dot_general(q_ref[pl.ds(row0, tq - row0), :], k_ref[pl.ds(ks, tk), :],
                               (((1,), (1,)), ((), ())), preferred_element_type=F32)

    def update(s, ki, row0, masked):
        rows = pl.ds(row0, tq - row0)
        ks = pl.multiple_of(ki * tk, tk)
        if masked:
            q_pos = qi * tq + row0 + lax.broadcasted_iota(jnp.int32, s.shape, 0)
            k_pos = ks + lax.broadcasted_iota(jnp.int32, s.shape, 1)
            s = jnp.where(k_pos <= q_pos, s, -jnp.inf)
        m_old = m_sc[rows, :]
        m_new = jnp.maximum(m_old, jnp.max(s, axis=-1, keepdims=True))
        alpha = jnp.exp2((m_old - m_new) * EXP2_SCALE)
        p = jnp.exp2((s - _lane_tile(m_new, tk)) * EXP2_SCALE)
        l_sc[rows, :] = alpha * l_sc[rows, :] + jnp.sum(p, axis=-1, keepdims=True)
        acc_sc[rows, :] = alpha * acc_sc[rows, :] + jnp.dot(p.astype(BF16), v_ref[pl.ds(ks, tk), :],
                                                            preferred_element_type=F32)
        m_sc[rows, :] = m_new

    group = tq // tk

    def full_group(gi, carry):
        ss = [scores(gi * group + j, 0) for j in range(group)]
        for j in range(group):
            update(ss[j], gi * group + j, 0, False)
        return carry

    lax.fori_loop(0, qi, full_group, 0)
    ss = [scores(qi * group + d, d * tk) for d in range(group)]
    for d in range(group):
        update(ss[d], qi * group + d, d * tk, True)
    o_ref[...] = (acc_sc[...] / l_sc[...] * g_ref[...].astype(F32)).astype(o_ref.dtype)


def _prompt_attn(q256, k256, v, gate, *, batch, t_pad, tq, tk):
    nq = t_pad // tq
    return pl.pallas_call(
        functools.partial(_prompt_attn_body, tq=tq, tk=tk),
        grid=(batch, N_HEADS, nq),
        in_specs=[
            pl.BlockSpec((tq, QK_PAD), lambda b, h, i: (b * nq + i, h)),
            pl.BlockSpec((t_pad, QK_PAD), lambda b, h, i: (b, h)),
            pl.BlockSpec((t_pad, V_HEAD), lambda b, h, i: (b, h)),
            pl.BlockSpec((tq, V_HEAD), lambda b, h, i: (b * nq + i, h)),
        ],
        out_specs=pl.BlockSpec((tq, V_HEAD), lambda b, h, i: (b * nq + i, h)),
        out_shape=jax.ShapeDtypeStruct((batch * t_pad, ATTN_WIDTH), BF16),
        scratch_shapes=[pltpu.VMEM((tq, LANES), F32), pltpu.VMEM((tq, LANES), F32),
                        pltpu.VMEM((tq, V_HEAD), F32)],
        compiler_params=_cparams(("parallel", "parallel", "arbitrary")),
        name="prompt_attn",
    )(q256, k256, v, gate)


def _sample_attn_body(pt_ref, ql_ref, qr_ref, cn_ref, kn_ref, c_hbm, rt_hbm, o_ref,
                      c_stage, rt_stage, c_sem, rt_sem, *scratch, layer, n_req, n_steps, n_pages, n_sub):
    cbufs, rtbufs = scratch[:n_sub], scratch[n_sub:2 * n_sub]
    m_sc, l_sc, acc_sc = scratch[2 * n_sub:]
    n_slots = c_stage.shape[0]
    depth = n_slots - 1
    step = pl.program_id(1)
    total = n_req * n_steps
    flat = pl.program_id(0) * n_steps + step
    slot = flat % n_slots

    def page_copies(f, lookup):
        sl = f % n_slots
        f = jnp.where(f >= total, f - total, f)
        r, s = f // n_steps, f % n_steps
        copies = []
        for k in range(n_pages):
            pg = pt_ref[r, s * n_pages + k] if lookup else 0
            copies.append(pltpu.make_async_copy(c_hbm.at[layer, pg], c_stage.at[sl, k], c_sem.at[sl]))
            copies.append(pltpu.make_async_copy(rt_hbm.at[layer, pg], rt_stage.at[sl, k], rt_sem.at[sl]))
        return copies

    @pl.when(flat == 0)
    def _():
        for d in range(depth):
            for cp in page_copies(flat + d, True):
                cp.start()

    ql = ql_ref[0]
    qr = qr_ref[0]
    dn_t = (((1,), (1,)), ((), ()))

    @pl.when(step == 0)
    def _():
        cn = cn_ref[0]
        kn = kn_ref[0]
        s = (lax.dot_general(ql, cn, dn_t, preferred_element_type=F32)
             + lax.dot_general(qr, kn, dn_t, preferred_element_type=F32))
        tok = lax.broadcasted_iota(jnp.int32, s.shape, 0) // N_HEADS
        key = lax.broadcasted_iota(jnp.int32, s.shape, 1)
        s = jnp.where(key <= tok, s, -jnp.inf)
        m = jnp.max(s, axis=-1, keepdims=True)
        p = jnp.exp2((s - m) * EXP2_SCALE)
        m_sc[...] = jnp.broadcast_to(m, m_sc.shape)
        l_sc[...] = jnp.broadcast_to(jnp.sum(p, axis=-1, keepdims=True), l_sc.shape)
        acc_sc[...] = jnp.dot(p.astype(BF16), cn, preferred_element_type=F32)

    for cp in page_copies(flat + depth, True):
        cp.start()
    for cp in page_copies(flat, False):
        cp.wait()

    per = n_pages // n_sub
    page = c_stage.shape[2]
    scores = []
    for sub in range(n_sub):
        for i in range(per):
            k = sub * per + i
            cbufs[sub][i * page:(i + 1) * page, :] = c_stage[slot, k].astype(BF16)
            rtbufs[sub][:, i * page:(i + 1) * page] = rt_stage[slot, k].astype(BF16)
        scores.append(lax.dot_general(ql, cbufs[sub][...], dn_t, preferred_element_type=F32)
                      + jnp.dot(qr, rtbufs[sub][...], preferred_element_type=F32))
    m_run, l_run, acc = m_sc[...], l_sc[...], acc_sc[...]
    for sub in range(n_sub):
        s = scores[sub]
        m_new = jnp.maximum(m_run, jnp.max(s, axis=-1, keepdims=True))
        alpha = jnp.exp2((m_run - m_new) * EXP2_SCALE)
        p = jnp.exp2((s - _lane_tile(m_new, per * page)) * EXP2_SCALE)
        l_run = alpha * l_run + jnp.sum(p, axis=-1, keepdims=True)
        acc = (_lane_tile(alpha, KV_RANK) * acc
               + jnp.dot(p.astype(BF16), cbufs[sub][...], preferred_element_type=F32))
        m_run = m_new
    m_sc[...], l_sc[...], acc_sc[...] = m_run, l_run, acc

    @pl.when(step + 1 == n_steps)
    def _():
        o_ref[0] = (acc / _lane_tile(l_run, KV_RANK)).astype(o_ref.dtype)

    @pl.when(flat + 1 == total)
    def _():
        for d in range(1, depth + 1):
            for cp in page_copies(flat + d, False):
                cp.wait()


def _sample_attn(page_table, q_lat, q_rope, cn_pad, kn_pad, cache_c, cache_rt, *, layer):
    nb, n_rows, _ = q_lat.shape
    n_tbl = page_table.shape[1]
    npg = PAGES_PER_STEP
    n_steps = n_tbl // npg
    page = cache_c.shape[2]

    def q_map(b, s, pt):
        return (b, 0, 0)

    in_specs = [
        pl.BlockSpec((1, n_rows, KV_RANK), q_map),
        pl.BlockSpec((1, n_rows, QK_ROPE), q_map),
        pl.BlockSpec((1, NEW_PAD, KV_RANK), q_map),
        pl.BlockSpec((1, NEW_PAD, QK_ROPE), q_map),
        pl.BlockSpec(memory_space=pl.ANY),
        pl.BlockSpec(memory_space=pl.ANY),
    ]
    grid_spec = pltpu.PrefetchScalarGridSpec(
        num_scalar_prefetch=1,
        grid=(nb, n_steps),
        in_specs=in_specs,
        out_specs=pl.BlockSpec((1, n_rows, KV_RANK), q_map),
        scratch_shapes=[
            pltpu.VMEM((SAMPLE_STAGE_SLOTS, npg, page, KV_RANK), F32),
            pltpu.VMEM((SAMPLE_STAGE_SLOTS, npg, QK_ROPE, page), F32),
            pltpu.SemaphoreType.DMA((SAMPLE_STAGE_SLOTS,)),
            pltpu.SemaphoreType.DMA((SAMPLE_STAGE_SLOTS,)),
            *[pltpu.VMEM((npg // SAMPLE_SUB_CHUNKS * page, KV_RANK), BF16)] * SAMPLE_SUB_CHUNKS,
            *[pltpu.VMEM((QK_ROPE, npg // SAMPLE_SUB_CHUNKS * page), BF16)] * SAMPLE_SUB_CHUNKS,
            pltpu.VMEM((n_rows, LANES), F32),
            pltpu.VMEM((n_rows, LANES), F32),
            pltpu.VMEM((n_rows, KV_RANK), F32),
        ],
    )
    return pl.pallas_call(
        functools.partial(_sample_attn_body, layer=layer, n_req=nb, n_steps=n_steps, n_pages=npg,
                          n_sub=SAMPLE_SUB_CHUNKS),
        grid_spec=grid_spec,
        out_shape=jax.ShapeDtypeStruct((nb, n_rows, KV_RANK), F32),
        compiler_params=_cparams(("arbitrary", "arbitrary")),
        name="sample_attn",
    )(page_table, q_lat, q_rope, cn_pad, kn_pad, cache_c, cache_rt)


CONV_HALO = 32
CONV_CHUNK = 128


def _prompt_conv_body(u_ref, w_ref, b_ref, o_ref, ext, *, t_pad):
    tc = u_ref.shape[1]
    ext[0:CONV_HALO, :] = jnp.zeros((CONV_HALO, tc), F32)
    ext[CONV_HALO:CONV_HALO + t_pad, :] = u_ref[...]
    ext[CONV_HALO + t_pad:, :] = jnp.zeros((SUBLANES, tc), F32)
    off = CONV_HALO - (CONV_K - 1)
    span = CONV_CHUNK + SUBLANES
    for c in range(t_pad // CONV_CHUNK):
        base = c * CONV_CHUNK
        acc = jnp.broadcast_to(b_ref[...], (CONV_CHUNK, tc))
        for r in range(SUBLANES):
            z = None
            for k in range(CONV_K):
                if (off + k) % SUBLANES != r:
                    continue
                a = base + off + k - r
                term = ext[a:a + span, :] * w_ref[k:k + 1, :]
                z = term if z is None else z + term
            acc = acc + z[r:r + CONV_CHUNK, :]
        o_ref[base:base + CONV_CHUNK, :] = acc


def _prompt_conv(u, w_dw, b_dw, *, batch, t_pad, tc):
    c = u.shape[1]
    return pl.pallas_call(
        functools.partial(_prompt_conv_body, t_pad=t_pad),
        grid=(batch, c // tc),
        in_specs=[
            pl.BlockSpec((t_pad, tc), lambda b, j: (b, j)),
            pl.BlockSpec((CONV_K, tc), lambda b, j: (0, j)),
            pl.BlockSpec((1, tc), lambda b, j: (0, j)),
        ],
        out_specs=pl.BlockSpec((t_pad, tc), lambda b, j: (b, j)),
        out_shape=jax.ShapeDtypeStruct(u.shape, F32),
        scratch_shapes=[pltpu.VMEM((CONV_HALO + t_pad + SUBLANES, tc), F32)],
        compiler_params=_cparams(("parallel", "parallel")),
        name="prompt_conv",
    )(u, w_dw, b_dw.reshape(1, c))


def _sample_conv_body(st_ref, u_ref, ws_ref, wu_ref, b_ref, o_ref, *, n_new):
    st = st_ref[...]
    u = u_ref[...]
    for t in range(n_new):
        y = jnp.sum(st * ws_ref[t][None], axis=1) + jnp.sum(u * wu_ref[t][None], axis=1)
        o_ref[:, t, :] = y + b_ref[...]


def _sample_conv(state, u, w_dw, b_dw, *, bb, tc):
    nb, hist, c = state.shape
    n_new = u.shape[1]
    w_ext = jnp.stack([jnp.pad(w_dw, ((t, n_new - 1 - t), (0, 0))) for t in range(n_new)])
    w_state = w_ext[:, :hist]
    w_new = w_ext[:, hist:]
    return pl.pallas_call(
        functools.partial(_sample_conv_body, n_new=n_new),
        grid=(nb // bb, c // tc),
        in_specs=[
            pl.BlockSpec((bb, hist, tc), lambda i, j: (i, 0, j)),
            pl.BlockSpec((bb, n_new, tc), lambda i, j: (i, 0, j)),
            pl.BlockSpec((n_new, hist, tc), lambda i, j: (0, 0, j)),
            pl.BlockSpec((n_new, n_new, tc), lambda i, j: (0, 0, j)),
            pl.BlockSpec((1, tc), lambda i, j: (0, j)),
        ],
        out_specs=pl.BlockSpec((bb, n_new, tc), lambda i, j: (i, 0, j)),
        out_shape=jax.ShapeDtypeStruct(u.shape, F32),
        compiler_params=_cparams(("parallel", "parallel")),
        name="sample_conv",
    )(state, u, w_state, w_new, b_dw.reshape(1, c))


def _merge_body(a_ref, conv_ref, sgc_ref, lnw_ref, lnb_ref, woa_ref, woc_ref, ga_ref, gc_ref, o_ref, cg):
    @pl.when(pl.program_id(1) == 0)
    def _():
        x = conv_ref[...]
        mu = jnp.mean(x, axis=-1, keepdims=True)
        xc = x - mu
        y = xc * lax.rsqrt(jnp.mean(xc * xc, axis=-1, keepdims=True) + NORM_EPS)
        y = y * lnw_ref[...] + lnb_ref[...]
        cg[...] = (y * jax.nn.sigmoid(y) * sgc_ref[...].astype(F32)).astype(BF16)

    ya = jnp.dot(a_ref[...], woa_ref[...], preferred_element_type=F32)
    yc = jnp.dot(cg[...], woc_ref[...], preferred_element_type=F32)
    o_ref[...] = (ga_ref[...].astype(F32) * ya + gc_ref[...].astype(F32) * yc).astype(o_ref.dtype)


def _merge(a, conv, sgc, ln_w, ln_b, w_oa, w_oc, g_a, g_c, *, tm, tn):
    rows, d = a.shape
    row_full = pl.BlockSpec((tm, d), lambda i, j: (i, 0))
    vec = pl.BlockSpec((1, d), lambda i, j: (0, 0))
    w_spec = pl.BlockSpec((d, tn), lambda i, j: (0, j))
    tile = pl.BlockSpec((tm, tn), lambda i, j: (i, j))
    return pl.pallas_call(
        _merge_body,
        grid=(rows // tm, d // tn),
        in_specs=[row_full, row_full, row_full, vec, vec, w_spec, w_spec, tile, tile],
        out_specs=tile,
        out_shape=jax.ShapeDtypeStruct((rows, d), BF16),
        scratch_shapes=[pltpu.VMEM((tm, d), BF16)],
        compiler_params=_cparams(("parallel", "arbitrary")),
        name="branch_merge",
    )(a, conv, sgc, ln_w.reshape(1, d), ln_b.reshape(1, d), w_oa, w_oc, g_a, g_c)


def _out_body(m_ref, w_ref, h_ref, g_ref, o_ref):
    y = jnp.dot(m_ref[...], w_ref[...], preferred_element_type=F32)
    yn = y * lax.rsqrt(jnp.mean(y * y, axis=-1, keepdims=True) + NORM_EPS) * g_ref[...]
    o_ref[...] = h_ref[...] + yn


def _out_proj(merged, w_out, h, g_post, *, tm):
    rows, d = h.shape
    row = pl.BlockSpec((tm, d), lambda i: (i, 0))
    return pl.pallas_call(
        _out_body,
        grid=(rows // tm,),
        in_specs=[row, pl.BlockSpec((d, d), lambda i: (0, 0)), row, pl.BlockSpec((1, d), lambda i: (0, 0))],
        out_specs=row,
        out_shape=jax.ShapeDtypeStruct((rows, d), F32),
        compiler_params=_cparams(("parallel",)),
        name="out_proj",
    )(merged, w_out, h, g_post.reshape(1, d))


def _rope_tables128(pos):
    inv = 1.0 / (ROPE_THETA ** (jnp.arange(0, QK_ROPE, 2, dtype=F32) / QK_ROPE))
    ang = pos.astype(F32)[:, None] * inv[None, :]
    cos, sin = jnp.cos(ang), jnp.sin(ang)
    return jnp.concatenate([cos, cos, cos, cos], axis=1), jnp.concatenate([-sin, sin, -sin, sin], axis=1)


def _split_w_in(w_in):
    a = Q_COLS
    b = a + KV_RANK
    c = b + QK_ROPE
    d = c + ATTN_WIDTH
    e = d + 2 * CONV_WIDTH
    f = e + CONV_WIDTH
    w_q, w_ckv, w_kr, w_ga, w_glu, w_gc, w_mg = jnp.split(w_in, [a, b, c, d, e, f], axis=1)
    dm = w_in.shape[0]
    w_q = jnp.pad(w_q.reshape(dm, N_HEADS, QK_HEAD), ((0, 0), (0, 0), (0, QK_PAD - QK_HEAD)))
    w_q = w_q.reshape(dm, N_HEADS * QK_PAD)
    w_kr = jnp.pad(w_kr, ((0, 0), (0, LANES - QK_ROPE)))
    w_glu_a, w_glu_b = jnp.split(w_glu, 2, axis=1)
    w_mga, w_mgc = jnp.split(w_mg, 2, axis=1)
    bf = lambda w: w.astype(BF16)
    return dict(q=bf(w_q), ckv=bf(w_ckv), kr=bf(w_kr), ga=bf(w_ga), glu_a=bf(w_glu_a), glu_b=bf(w_glu_b),
                gc=bf(w_gc), mga=bf(w_mga), mgc=bf(w_mgc))


def _project_all(h, pos, w, g_pre, g_kv, *, tm):
    hn = _rmsnorm(h, g_pre, tm)
    cos_t, sin_t = _rope_tables128(pos)
    hpt = 4
    q256, = _proj(hn, [w["q"]], functools.partial(_ep_q, hpt), [(N_HEADS * QK_PAD, hpt * QK_PAD, BF16)],
                  tm=tm, tns=[hpt * QK_PAD], row_aux=(cos_t, sin_t), name="proj_q")
    ckv, kr, kr_pad = _proj(hn, [w["ckv"], w["kr"]], _ep_ckv_kr,
                            [(KV_RANK, KV_RANK, F32), (QK_ROPE, QK_ROPE, F32), (LANES, LANES, BF16)],
                            tm=tm, tns=[KV_RANK, LANES], vec_aux=(g_kv.reshape(1, KV_RANK),),
                            row_aux=(cos_t, sin_t), name="proj_ckv_kr")
    sga, = _proj(hn, [w["ga"]], _ep_silu, [(ATTN_WIDTH, 1024, BF16)], tm=tm, tns=[1024], name="proj_gate_a")
    u, = _proj(hn, [w["glu_a"], w["glu_b"]], _ep_glu, [(CONV_WIDTH, 1024, F32)], tm=tm, tns=[1024, 1024],
               name="proj_glu")
    sgc, = _proj(hn, [w["gc"]], _ep_silu, [(CONV_WIDTH, 1024, BF16)], tm=tm, tns=[1024], name="proj_gate_c")
    g_a, g_c = _proj(hn, [w["mga"], w["mgc"]], _ep_sigmoid2, [(D_MODEL, 1024, BF16), (D_MODEL, 1024, BF16)],
                     tm=tm, tns=[1024, 1024], name="proj_merge_gates")
    return q256, ckv, kr, kr_pad, sga, u, sgc, g_a, g_c


def _finish(a, conv, sgc, g_a, g_c, h, lw, *, tm_merge, tm_out):
    merged = _merge(a, conv, sgc, lw["ln_w"], lw["ln_b"], lw["w_oa"], lw["w_oc"], g_a, g_c, tm=tm_merge, tn=512)
    return _out_proj(merged, lw["w_out"], h, lw["g_post"], tm=tm_out)


def kernel(x_prompt, x_sample, cache_ckv, cache_krope, state_conv, page_table, meta_tokens, w_norm_pre,
           w_norm_post, w_in, w_kv_norm, w_uk, w_uv, w_o_attn, w_dw, b_dw, w_conv_ln, b_conv_ln, w_o_conv,
           w_out):
    batch, seq, d = x_prompt.shape
    nb, n_new, _ = x_sample.shape
    depth = w_in.shape[0]
    t_real = seq + N_META
    t_pad = -(-t_real // T_TILE) * T_TILE
    past_len = page_table.shape[1] * cache_ckv.shape[2]
    hist = CONV_K - 1

    meta = jnp.broadcast_to(meta_tokens[None].astype(x_prompt.dtype), (batch, N_META, d))
    hp = jnp.concatenate([meta, x_prompt, jnp.zeros((batch, t_pad - t_real, d), x_prompt.dtype)], axis=1)
    hp = hp.reshape(batch * t_pad, d)
    hs = x_sample.reshape(nb * n_new, d)
    pos_p = jnp.arange(t_pad)
    pos_s = jnp.tile(past_len + jnp.arange(n_new), nb)
    tm_p, tm_s = 768, nb * n_new
    cache_krope_t = jnp.swapaxes(cache_krope, 2, 3)

    outs = [[] for _ in range(6)]
    for l in range(depth):
        w = _split_w_in(w_in[l])
        w_uk_l = w_uk[l].reshape(KV_RANK, N_HEADS * QK_NOPE).astype(BF16)
        w_uv_l = w_uv[l].reshape(KV_RANK, N_HEADS * V_HEAD).astype(BF16)
        w_uk_t = jnp.transpose(w_uk[l], (1, 2, 0)).astype(BF16)
        w_uv_h = jnp.transpose(w_uv[l], (1, 0, 2)).astype(BF16)
        lw = dict(ln_w=w_conv_ln[l], ln_b=b_conv_ln[l], w_oa=w_o_attn[l].astype(BF16),
                  w_oc=w_o_conv[l].astype(BF16), w_out=w_out[l].astype(BF16), g_post=w_norm_post[l])

        q256, ckv_p, kr_p, kr_pad, sga, u_p, sgc, g_a, g_c = _project_all(
            hp, pos_p, w, w_norm_pre[l], w_kv_norm[l], tm=tm_p)
        k256, v = _proj(ckv_p, [w_uk_l, w_uv_l], functools.partial(_ep_kv_up, 8),
                        [(N_HEADS * QK_PAD, 8 * QK_PAD, BF16), (ATTN_WIDTH, 8 * V_HEAD, BF16)],
                        tm=tm_p, tns=[8 * QK_NOPE, 8 * V_HEAD], row_aux=(kr_pad,), name="kv_up")
        a_p = _prompt_attn(q256, k256, v, sga, batch=batch, t_pad=t_pad, tq=tm_p, tk=T_TILE)
        conv_p = _prompt_conv(u_p, w_dw[l], b_dw[l], batch=batch, t_pad=t_pad, tc=LANES)
        hp_new = _finish(a_p, conv_p, sgc, g_a, g_c, hp, lw, tm_merge=tm_p, tm_out=512)

        q256, ckv_s, kr_s, _, sga, u_s, sgc, g_a, g_c = _project_all(
            hs, pos_s, w, w_norm_pre[l], w_kv_norm[l], tm=tm_s)
        q4 = q256.reshape(nb * n_new, N_HEADS, QK_PAD)
        q_nope = q4[:, :, :QK_NOPE].reshape(nb * n_new, N_HEADS * QK_NOPE)
        q_rope = q4[:, :, QK_NOPE:QK_HEAD].reshape(nb, n_new * N_HEADS, QK_ROPE)
        q_lat = _head_mm(q_nope, w_uk_t, out_dtype=BF16, name="q_absorb")
        q_lat = q_lat.reshape(nb, n_new * N_HEADS, KV_RANK)
        pad_new = lambda x: jnp.pad(x.reshape(nb, n_new, -1), ((0, 0), (0, NEW_PAD - n_new), (0, 0))).astype(BF16)
        o_lat = _sample_attn(page_table, q_lat, q_rope, pad_new(ckv_s), pad_new(kr_s),
                             cache_ckv, cache_krope_t, layer=l)
        a_s = _head_mm(o_lat.reshape(nb * n_new, N_HEADS * KV_RANK), w_uv_h, sga, out_dtype=BF16, name="v_up")
        conv_s = _sample_conv(state_conv[l], u_s.reshape(nb, n_new, d), w_dw[l], b_dw[l], bb=8, tc=512)
        hs_new = _finish(a_s, conv_s.reshape(nb * n_new, d), sgc, g_a, g_c, hs, lw, tm_merge=tm_s, tm_out=tm_s)

        unpad = lambda x: x.reshape(batch, t_pad, -1)[:, :t_real]
        outs[0].append(unpad(ckv_p))
        outs[1].append(unpad(kr_p))
        outs[2].append(unpad(u_p)[:, t_real - hist:])
        outs[3].append(ckv_s.reshape(nb, n_new, KV_RANK))
        outs[4].append(kr_s.reshape(nb, n_new, QK_ROPE))
        outs[5].append(jnp.concatenate([state_conv[l], u_s.reshape(nb, n_new, d)], axis=1)[:, n_new:])
        hp, hs = hp_new, hs_new

    y_prompt = hp.reshape(batch, t_pad, d)[:, N_META:t_real]
    y_sample = hs.reshape(nb, n_new, d)
    return (y_prompt, y_sample) + tuple(jnp.stack(o) for o in outs)
```
